```python
import jax, jax.numpy as jnp
from jax import lax
import numpy as np

D_MODEL = 4096
BATCH = 1
SEQ = 8192
DEPTH = 1

CHUNK = 64
HEAD_DIM = 128
N_HEADS_FOX = 16
N_HEADS_SB = 16
W_FOX = N_HEADS_FOX * HEAD_DIM
W_SB = N_HEADS_SB * HEAD_DIM
D_FF = 11008
CONV_WIDTH = 3
PLE_DIM = 256
Q_BLOCK = 128
EPS = 1e-6
IN_SPLITS = (W_FOX, W_FOX, W_FOX, N_HEADS_FOX, W_SB, W_SB, W_SB, D_MODEL, D_MODEL)
N_IN = sum(IN_SPLITS)
SPLIT_POINTS = tuple(int(v) for v in np.cumsum(IN_SPLITS)[:-1])

kernel_name = "fox_stickbreak_gated_hybrid_block"


def _rms_norm(x, g):
    x32 = x.astype(jnp.float32)
    y = x32 * lax.rsqrt(jnp.mean(x32 * x32, axis=-1, keepdims=True) + EPS)
    return (y * g.astype(jnp.float32)).astype(x.dtype)


def _to_blocks(t):
    B, S = t.shape[:2]
    t = t.reshape((B, S // Q_BLOCK, Q_BLOCK) + t.shape[2:])
    return jnp.moveaxis(t, 1, 0)


def _from_blocks(o):
    nb, B, Q, H, d = o.shape
    return jnp.moveaxis(o, 0, 1).reshape(B, nb * Q, H * d)


def _forgetting_attention(q, k, v, log_f):
    B, S, H, d = q.shape
    scale = d ** -0.5
    c = jnp.cumsum(log_f, axis=1)
    c_key = jnp.transpose(c, (0, 2, 1))
    q_blocks = _to_blocks(q)
    c_blocks = jnp.transpose(_to_blocks(c), (0, 1, 3, 2))
    key_pos = jnp.arange(S)

    def one_block(args):
        q_i, c_i, i = args
        q_pos = i * Q_BLOCK + jnp.arange(Q_BLOCK)
        s = jnp.einsum('bqhd,bkhd->bhqk', q_i, k,
                       preferred_element_type=jnp.float32) * scale
        s = s + c_i[..., :, None] - c_key[:, :, None, :]
        causal = key_pos[None, :] <= q_pos[:, None]
        s = jnp.where(causal, s, -jnp.inf)
        w = jax.nn.softmax(s, axis=-1)
        return jnp.einsum('bhqk,bkhd->bqhd', w.astype(v.dtype), v)

    out = lax.map(one_block, (q_blocks, c_blocks, jnp.arange(S // Q_BLOCK)))
    return _from_blocks(out)


def _stick_breaking_attention(q, k, v):
    B, S, H, d = q.shape
    scale = d ** -0.5
    q_blocks = _to_blocks(q)
    key_pos = jnp.arange(S)

    def one_block(args):
        q_i, i = args
        q_pos = i * Q_BLOCK + jnp.arange(Q_BLOCK)
        z = jnp.einsum('bqhd,bkhd->bhqk', q_i, k,
                       preferred_element_type=jnp.float32) * scale
        strict = key_pos[None, :] < q_pos[:, None]
        log_beta = jax.nn.log_sigmoid(z)
        log_one_minus = jnp.where(strict, log_beta - z, 0.0)
        tail = lax.cumsum(log_one_minus, axis=3, reverse=True) - log_one_minus
        a = jnp.where(strict, jnp.exp(log_beta + tail), 0.0)
        return jnp.einsum('bhqk,bkhd->bqhd', a.astype(v.dtype), v)

    out = lax.map(one_block, (q_blocks, jnp.arange(S // Q_BLOCK)))
    return _from_blocks(out)


def _causal_depthwise_conv(u, w, b):
    S = u.shape[1]
    u_pad = jnp.pad(u, ((0, 0), (CONV_WIDTH - 1, 0), (0, 0)))
    out = b
    for j in range(CONV_WIDTH):
        out = out + w[j] * u_pad[:, j:j + S]
    return out


def setup_inputs(seed: int = 0) -> dict:
    key = jax.random.key(seed)
    ks = jax.random.split(key, 18)
    f32 = jnp.float32

    def nrm(k, shape, scale):
        return jax.random.normal(k, shape, f32) * scale

    def gain(k, shape):
        return 1.0 + 0.05 * jax.random.normal(k, shape, f32)

    return {
        "x": nrm(ks[0], (BATCH, SEQ, D_MODEL), 1.0),
        "p": nrm(ks[1], (DEPTH, BATCH, SEQ, PLE_DIM), 1.0),
        "g_mix": gain(ks[2], (DEPTH, D_MODEL)),
        "w_in": nrm(ks[3], (DEPTH, D_MODEL, N_IN), D_MODEL ** -0.5),
        "b_f": 3.0 + 0.5 * jax.random.normal(ks[4], (DEPTH, N_HEADS_FOX), f32),
        "g_q_fox": gain(ks[5], (DEPTH, HEAD_DIM)),
        "g_k_fox": gain(ks[6], (DEPTH, HEAD_DIM)),
        "w_branch_fox": nrm(ks[7], (DEPTH, W_FOX, D_MODEL), W_FOX ** -0.5),
        "w_branch_sb": nrm(ks[8], (DEPTH, W_SB, D_MODEL), W_SB ** -0.5),
        "w_out": nrm(ks[9], (DEPTH, D_MODEL, D_MODEL), D_MODEL ** -0.5),
        "g_ffn": gain(ks[10], (DEPTH, D_MODEL)),
        "w_up": nrm(ks[11], (DEPTH, D_MODEL, 2 * D_FF), D_MODEL ** -0.5),
        "conv_w": nrm(ks[12], (DEPTH, CONV_WIDTH, 2 * D_FF), CONV_WIDTH ** -0.5),
        "conv_b": nrm(ks[13], (DEPTH, 2 * D_FF), 0.01),
        "w_down": nrm(ks[14], (DEPTH, D_FF, D_MODEL), D_FF ** -0.5),
        "g_ple": gain(ks[15], (DEPTH, D_MODEL)),
        "w_ple_gate": nrm(ks[16], (DEPTH, D_MODEL, D_MODEL), D_MODEL ** -0.5),
        "w_ple_proj": nrm(ks[17], (DEPTH, PLE_DIM, D_MODEL), PLE_DIM ** -0.5),
    }


def reference(x, p, g_mix, w_in, b_f, g_q_fox, g_k_fox, w_branch_fox, w_branch_sb,
              w_out, g_ffn, w_up, conv_w, conv_b, w_down, g_ple, w_ple_gate, w_ple_proj):
    B, S, _ = x.shape
    for i in range(DEPTH):
        h = _rms_norm(x, g_mix[i])
        proj = h @ w_in[i]
        q_a, k_a, v_a, f_a, q_b, k_b, v_b, gate_a, gate_b = jnp.split(proj, SPLIT_POINTS, axis=-1)

        q_a = _rms_norm(q_a.reshape(B, S, N_HEADS_FOX, HEAD_DIM), g_q_fox[i])
        k_a = _rms_norm(k_a.reshape(B, S, N_HEADS_FOX, HEAD_DIM), g_k_fox[i])
        v_a = v_a.reshape(B, S, N_HEADS_FOX, HEAD_DIM)
        log_f = jax.nn.log_sigmoid((f_a + b_f[i]).astype(jnp.float32))
        y_a = _forgetting_attention(q_a, k_a, v_a, log_f)

        q_b = q_b.reshape(B, S, N_HEADS_SB, HEAD_DIM)
        k_b = k_b.reshape(B, S, N_HEADS_SB, HEAD_DIM)
        v_b = v_b.reshape(B, S, N_HEADS_SB, HEAD_DIM)
        y_b = _stick_breaking_attention(q_b, k_b, v_b)

        merged = (jax.nn.sigmoid(gate_a) * (y_a @ w_branch_fox[i])
                  + jax.nn.sigmoid(gate_b) * (y_b @ w_branch_sb[i]))
        x = x + merged @ w_out[i]

        h = _rms_norm(x, g_ffn[i])
        u = _causal_depthwise_conv(h @ w_up[i], conv_w[i], conv_b[i])
        u_gate, u_val = jnp.split(u, 2, axis=-1)
        x = x + (jax.nn.silu(u_gate) * u_val) @ w_down[i]

        ple_gate = jax.nn.sigmoid(_rms_norm(x, g_ple[i]) @ w_ple_gate[i])
        x = x + ple_gate * (p[i] @ w_ple_proj[i])
    return x
```

```python
import functools

import jax
import jax.numpy as jnp
from jax import lax
from jax.experimental import pallas as pl
from jax.experimental.pallas import tpu as pltpu

F32 = jnp.float32
BF16 = jnp.bfloat16

EPS = 1e-6
HEAD_DIM = 128
N_HEADS = 16
W_ATT = N_HEADS * HEAD_DIM
LANES = 128
SUBLANES = 8
CONV_WIDTH = 3
VMEM_LIMIT_BYTES = 56 * 1024 * 1024
UNDERFLOW = 105.0

COL_QA, COL_KA, COL_VA = 0, 16, 32
COL_QB, COL_KB, COL_VB = 48, 64, 80
COL_GA, COL_GB = 96, 128
N_PROJ = 160 * LANES


def _params(*sem):
    return pltpu.CompilerParams(dimension_semantics=sem, vmem_limit_bytes=VMEM_LIMIT_BYTES)


def _log_sigmoid(z):
    return jnp.minimum(z, 0.0) - jnp.log1p(jnp.exp(-jnp.abs(z)))


def _rms_norm_rows(x, g):
    ms = jnp.mean(x * x, axis=-1, keepdims=True)
    return x * lax.rsqrt(ms + EPS) * g


def _dot(a, b):
    return jnp.dot(a, b, preferred_element_type=F32)


def _dot_nt(a, b):
    return lax.dot_general(a, b, (((1,), (1,)), ((), ())), preferred_element_type=F32)


def _norm_proj_kernel(x_ref, g_ref, w_ref, wf_ref, bf_ref, gq_ref, gk_ref,
                      o_ref, lf_ref, h_ref, *, tn):
    j = pl.program_id(1)
    tiles = W_ATT // tn

    @pl.when(j == 0)
    def _():
        h = _rms_norm_rows(x_ref[...], g_ref[...]).astype(BF16)
        h_ref[...] = h
        lf_ref[...] = _log_sigmoid(_dot(h, wf_ref[...]) + bf_ref[...])

    acc = _dot(h_ref[...], w_ref[...])
    scale = HEAD_DIM ** -0.5

    def head_norm(g, mult):
        for c in range(tn // HEAD_DIM):
            sl = slice(c * HEAD_DIM, (c + 1) * HEAD_DIM)
            o_ref[:, sl] = (_rms_norm_rows(acc[:, sl], g) * mult).astype(BF16)

    @pl.when(j < tiles)
    def _():
        head_norm(gq_ref[...], scale)

    @pl.when((j >= tiles) & (j < 2 * tiles))
    def _():
        head_norm(gk_ref[...], 1.0)

    @pl.when((j >= 3 * tiles) & (j < 4 * tiles))
    def _():
        o_ref[...] = (acc * scale).astype(BF16)

    @pl.when(((j >= 2 * tiles) & (j < 3 * tiles)) | ((j >= 4 * tiles) & (j < 6 * tiles)))
    def _():
        o_ref[...] = acc.astype(BF16)

    @pl.when(j >= 6 * tiles)
    def _():
        o_ref[...] = jax.nn.sigmoid(acc).astype(BF16)


def _norm_proj(x, g, w_main, w_f, b_f, gq, gk, *, tm, tn):
    S, D = x.shape
    return pl.pallas_call(
        functools.partial(_norm_proj_kernel, tn=tn),
        grid=(S // tm, N_PROJ // tn),
        in_specs=[
            pl.BlockSpec((tm, D), lambda i, j: (i, 0)),
            pl.BlockSpec((1, D), lambda i, j: (0, 0)),
            pl.BlockSpec((D, tn), lambda i, j: (0, j)),
            pl.BlockSpec((D, LANES), lambda i, j: (0, 0)),
            pl.BlockSpec((1, LANES), lambda i, j: (0, 0)),
            pl.BlockSpec((1, HEAD_DIM), lambda i, j: (0, 0)),
            pl.BlockSpec((1, HEAD_DIM), lambda i, j: (0, 0)),
        ],
        out_specs=[
            pl.BlockSpec((tm, tn), lambda i, j: (i, j)),
            pl.BlockSpec((tm, LANES), lambda i, j: (i, 0)),
        ],
        out_shape=[
            jax.ShapeDtypeStruct((S, N_PROJ), BF16),
            jax.ShapeDtypeStruct((S, LANES), F32),
        ],
        scratch_shapes=[pltpu.VMEM((tm, D), BF16)],
        compiler_params=_params("parallel", "arbitrary"),
        name="norm_proj",
    )(x, g, w_main, w_f, b_f, gq, gk)


def _cumsum_kernel(x_ref, o_ref, carry_ref, *, tb):
    @pl.when(pl.program_id(0) == 0)
    def _():
        carry_ref[...] = jnp.zeros_like(carry_ref)

    x = x_ref[...]
    r = lax.broadcasted_iota(jnp.int32, (tb, tb), 0)
    c = lax.broadcasted_iota(jnp.int32, (tb, tb), 1)
    tri = (c <= r).astype(BF16)
    x1 = x.astype(BF16)
    r1 = x - x1.astype(F32)
    x2 = r1.astype(BF16)
    x3 = (r1 - x2.astype(F32)).astype(BF16)
    cs = _dot(tri, x1) + _dot(tri, x2) + _dot(tri, x3) + carry_ref[...]
    o_ref[...] = cs
    carry_ref[...] = cs[tb - 1:tb, :]


def _cumsum(x, *, tb):
    S, W = x.shape
    return pl.pallas_call(
        functools.partial(_cumsum_kernel, tb=tb),
        grid=(S // tb,),
        in_specs=[pl.BlockSpec((tb, W), lambda i: (i, 0))],
        out_specs=pl.BlockSpec((tb, W), lambda i: (i, 0)),
        out_shape=jax.ShapeDtypeStruct((S, W), F32),
        scratch_shapes=[pltpu.VMEM((1, W), F32)],
        compiler_params=_params("arbitrary"),
        name="cumsum_logf",
    )(x)


def _fox_kernel(q_ref, k_ref, v_ref, c_ref, o_ref, m_ref, l_ref, acc_ref, *, tq, tk):
    i = pl.program_id(1)
    nd = tq // tk
    q = q_ref[...]
    m_ref[...] = jnp.full_like(m_ref, -1e30)
    l_ref[...] = jnp.zeros_like(l_ref)
    acc_ref[...] = jnp.zeros_like(acc_ref)
    row = i * tq + lax.broadcasted_iota(jnp.int32, (tq, tk), 0)
    col = lax.broadcasted_iota(jnp.int32, (tq, tk), 1)

    def step(kb, masked):
        start = pl.multiple_of(kb * tk, tk)
        s = _dot_nt(q, k_ref[pl.ds(start, tk), :]) - c_ref[kb]
        if masked:
            s = jnp.where(start + col <= row, s, -jnp.inf)
        m_old = m_ref[...]
        m_new = jnp.maximum(m_old, jnp.max(s, axis=-1, keepdims=True))
        alpha = jnp.exp(m_old - m_new)
        p = jnp.exp(s - m_new)
        l_ref[...] = alpha * l_ref[...] + jnp.sum(p, axis=-1, keepdims=True)
        acc_ref[...] = alpha * acc_ref[...] + _dot(p.astype(BF16), v_ref[pl.ds(start, tk), :])
        m_ref[...] = m_new

    for d in range(nd):
        step((i + 1) * nd - 1 - d, True)

    def body(t, carry):
        step(i * nd - 1 - t, False)
        return carry

    lax.fori_loop(0, i * nd, body, 0)
    o_ref[...] = (acc_ref[...] / l_ref[...]).astype(BF16)


def _fox_attn(proj, c_rows, *, tq, tk):
    S = proj.shape[0]
    nkb = S // tk
    return pl.pallas_call(
        functools.partial(_fox_kernel, tq=tq, tk=tk),
        grid=(N_HEADS, S // tq),
        in_specs=[
            pl.BlockSpec((tq, HEAD_DIM), lambda h, i: (i, COL_QA + h)),
            pl.BlockSpec((S, HEAD_DIM), lambda h, i: (0, COL_KA + h)),
            pl.BlockSpec((S, HEAD_DIM), lambda h, i: (0, COL_VA + h)),
            pl.BlockSpec((None, nkb, 1, tk), lambda h, i: (h, 0, 0, 0)),
        ],
        out_specs=pl.BlockSpec((tq, HEAD_DIM), lambda h, i: (i, h)),
        out_shape=jax.ShapeDtypeStruct((S, W_ATT), BF16),
        scratch_shapes=[
            pltpu.VMEM((tq, 1), F32),
            pltpu.VMEM((tq, 1), F32),
            pltpu.VMEM((tq, HEAD_DIM), F32),
        ],
        compiler_params=_params("parallel", "arbitrary"),
        name="fox_attn",
    )(proj, proj, proj, c_rows)


def _sb_kernel(q_ref, k_ref, v_ref, o_ref, run_ref, acc_ref, *, tq, tk):
    i = pl.program_id(1)
    nd = tq // tk
    q = q_ref[...]
    run_ref[...] = jnp.zeros_like(run_ref)
    acc_ref[...] = jnp.zeros_like(acc_ref)
    row = i * tq + lax.broadcasted_iota(jnp.int32, (tq, tk), 0)
    col = lax.broadcasted_iota(jnp.int32, (tq, tk), 1)
    jj = lax.broadcasted_iota(jnp.int32, (tk, tk), 0)
    ss = lax.broadcasted_iota(jnp.int32, (tk, tk), 1)
    later = (jj > ss).astype(BF16)

    def cond(carry):
        kb, run_max = carry
        return (kb >= 0) & (run_max > -UNDERFLOW)

    def body(carry):
        kb, _ = carry
        start = pl.multiple_of(kb * tk, tk)
        z = _dot_nt(q, k_ref[pl.ds(start, tk), :])
        strict = start + col < row
        log_beta = _log_sigmoid(z)
        l1m = jnp.where(strict, log_beta - z, 0.0)
        hi = l1m.astype(BF16)
        lo = (l1m - hi.astype(F32)).astype(BF16)
        tail = _dot(hi, later) + _dot(lo, later)
        run = run_ref[...]
        a = jnp.where(strict, jnp.exp(log_beta + tail + run), 0.0)
        acc_ref[...] += _dot(a.astype(BF16), v_ref[pl.ds(start, tk), :])
        run_new = run + tail[:, :1] + l1m[:, :1]
        run_ref[...] = run_new
        return kb - 1, jnp.max(run_new)

    lax.while_loop(cond, body, ((i + 1) * nd - 1, jnp.float32(0.0)))
    o_ref[...] = acc_ref[...].astype(BF16)


def _sb_attn(proj, *, tq, tk):
    S = proj.shape[0]
    return pl.pallas_call(
        functools.partial(_sb_kernel, tq=tq, tk=tk),
        grid=(N_HEADS, S // tq),
        in_specs=[
            pl.BlockSpec((tq, HEAD_DIM), lambda h, i: (i, COL_QB + h)),
            pl.BlockSpec((S, HEAD_DIM), lambda h, i: (0, COL_KB + h)),
            pl.BlockSpec((S, HEAD_DIM), lambda h, i: (0, COL_VB + h)),
        ],
        out_specs=pl.BlockSpec((tq, HEAD_DIM), lambda h, i: (i, h)),
        out_shape=jax.ShapeDtypeStruct((S, W_ATT), BF16),
        scratch_shapes=[
            pltpu.VMEM((tq, 1), F32),
            pltpu.VMEM((tq, HEAD_DIM), F32),
        ],
        compiler_params=_params("parallel", "arbitrary"),
        name="sb_attn",
    )(proj, proj, proj)


def _merge_kernel(ya_ref, yb_ref, wa_ref, wb_ref, ga_ref, gb_ref, o_ref):
    a = _dot(ya_ref[...], wa_ref[...])
    b = _dot(yb_ref[...], wb_ref[...])
    o_ref[...] = (ga_ref[...].astype(F32) * a + gb_ref[...].astype(F32) * b).astype(BF16)


def _merge(y_a, y_b, w_a, w_b, proj, *, tm, tn):
    S, K = y_a.shape
    N = w_a.shape[1]
    ga0 = COL_GA * LANES // tn
    gb0 = COL_GB * LANES // tn
    return pl.pallas_call(
        _merge_kernel,
        grid=(S // tm, N // tn),
        in_specs=[
            pl.BlockSpec((tm, K), lambda i, j: (i, 0)),
            pl.BlockSpec((tm, K), lambda i, j: (i, 0)),
            pl.BlockSpec((K, tn), lambda i, j: (0, j)),
            pl.BlockSpec((K, tn), lambda i, j: (0, j)),
            pl.BlockSpec((tm, tn), lambda i, j: (i, ga0 + j)),
            pl.BlockSpec((tm, tn), lambda i, j: (i, gb0 + j)),
        ],
        out_specs=pl.BlockSpec((tm, tn), lambda i, j: (i, j)),
        out_shape=jax.ShapeDtypeStruct((S, N), BF16),
        compiler_params=_params("parallel", "arbitrary"),
        name="merge",
    )(y_a, y_b, w_a, w_b, proj, proj)


def _matmul_res_kernel(a_ref, w_ref, x_ref, o_ref):
    o_ref[...] = x_ref[...] + _dot(a_ref[...], w_ref[...])


def _matmul_res(a, w, x, *, tm, tn):
    S, K = a.shape
    N = w.shape[1]
    return pl.pallas_call(
        _matmul_res_kernel,
        grid=(S // tm, N // tn),
        in_specs=[
            pl.BlockSpec((tm, K), lambda i, j: (i, 0)),
            pl.BlockSpec((K, tn), lambda i, j: (0, j)),
            pl.BlockSpec((tm, tn), lambda i, j: (i, j)),
        ],
        out_specs=pl.BlockSpec((tm, tn), lambda i, j: (i, j)),
        out_shape=jax.ShapeDtypeStruct((S, N), F32),
        compiler_params=_params("parallel", "arbitrary"),
        name="out_res",
    )(a, w, x)


def _up_glu_kernel(x_ref, g_ref, wg_ref, wv_ref, cwg_ref, cwv_ref, cbg_ref, cbv_ref,
                   o_ref, h_ref, ug_ref, uv_ref, carry_g_ref, carry_v_ref, *, tm):
    i = pl.program_id(0)
    j = pl.program_id(1)
    halo = SUBLANES

    @pl.when(j == 0)
    def _():
        h_ref[...] = _rms_norm_rows(x_ref[...], g_ref[...]).astype(BF16)

    h = h_ref[...]

    def conv(w_ref, cw_ref, cb_ref, u_ref, carry_ref):
        u = _dot(h, w_ref[...])

        @pl.when(i == 0)
        def _():
            u_ref[0:halo, :] = jnp.zeros((halo, u.shape[1]), F32)

        @pl.when(i > 0)
        def _():
            u_ref[0:halo, :] = carry_ref[j]

        u_ref[halo:halo + tm, :] = u
        carry_ref[j] = u[tm - halo:tm, :]
        cw = cw_ref[...]
        return (cb_ref[...]
                + cw[0:1, :] * u_ref[halo - 2:halo - 2 + tm, :]
                + cw[1:2, :] * u_ref[halo - 1:halo - 1 + tm, :]
                + cw[2:3, :] * u)

    gate = conv(wg_ref, cwg_ref, cbg_ref, ug_ref, carry_g_ref)
    val = conv(wv_ref, cwv_ref, cbv_ref, uv_ref, carry_v_ref)
    o_ref[...] = (gate * jax.nn.sigmoid(gate) * val).astype(BF16)


def _up_glu(x, g, w_g, w_v, cw_g, cw_v, cb_g, cb_v, *, tm, tn):
    S, D = x.shape
    NF = w_g.shape[1]
    nj = NF // tn
    wspec = pl.BlockSpec((D, tn), lambda i, j: (0, j))
    cwspec = pl.BlockSpec((CONV_WIDTH, tn), lambda i, j: (0, j))
    cbspec = pl.BlockSpec((1, tn), lambda i, j: (0, j))
    return pl.pallas_call(
        functools.partial(_up_glu_kernel, tm=tm),
        grid=(S // tm, nj),
        in_specs=[
            pl.BlockSpec((tm, D), lambda i, j: (i, 0)),
            pl.BlockSpec((1, D), lambda i, j: (0, 0)),
            wspec, wspec, cwspec, cwspec, cbspec, cbspec,
        ],
        out_specs=pl.BlockSpec((tm, tn), lambda i, j: (i, j)),
        out_shape=jax.ShapeDtypeStruct((S, NF), BF16),
        scratch_shapes=[
            pltpu.VMEM((tm, D), BF16),
            pltpu.VMEM((tm + SUBLANES, tn), F32),
            pltpu.VMEM((tm + SUBLANES, tn), F32),
            pltpu.VMEM((nj, SUBLANES, tn), F32),
            pltpu.VMEM((nj, SUBLANES, tn), F32),
        ],
        compiler_params=_params("arbitrary", "arbitrary"),
        name="up_glu",
    )(x, g, w_g, w_v, cw_g, cw_v, cb_g, cb_v)


def _down_res_kernel(a_ref, w_ref, x_ref, o_ref, acc_ref):
    k = pl.program_id(2)

    @pl.when(k == 0)
    def _():
        acc_ref[...] = x_ref[...]

    acc_ref[...] += _dot(a_ref[...], w_ref[...])

    @pl.when(k == pl.num_programs(2) - 1)
    def _():
        o_ref[...] = acc_ref[...]


def _down_res(a, w, x, *, tm, tn, tk):
    S, K = a.shape
    N = w.shape[1]
    return pl.pallas_call(
        _down_res_kernel,
        grid=(S // tm, N // tn, K // tk),
        in_specs=[
            pl.BlockSpec((tm, tk), lambda i, j, k: (i, k)),
            pl.BlockSpec((tk, tn), lambda i, j, k: (k, j)),
            pl.BlockSpec((tm, tn), lambda i, j, k: (i, j)),
        ],
        out_specs=pl.BlockSpec((tm, tn), lambda i, j, k: (i, j)),
        out_shape=jax.ShapeDtypeStruct((S, N), F32),
        scratch_shapes=[pltpu.VMEM((tm, tn), F32)],
        compiler_params=_params("parallel", "parallel", "arbitrary"),
        name="down_res",
    )(a, w, x)


def _ple_kernel(x_ref, g_ref, wg_ref, p_ref, wp_ref, xr_ref, o_ref, h_ref):
    @pl.when(pl.program_id(1) == 0)
    def _():
        h_ref[...] = _rms_norm_rows(x_ref[...], g_ref[...]).astype(BF16)

    gate = jax.nn.sigmoid(_dot(h_ref[...], wg_ref[...]))
    emb = _dot(p_ref[...].astype(BF16), wp_ref[...])
    o_ref[...] = xr_ref[...] + gate * emb


def _ple(x, g, w_gate, p, w_proj, *, tm, tn):
    S, D = x.shape
    P = p.shape[1]
    return pl.pallas_call(
        _ple_kernel,
        grid=(S // tm, D // tn),
        in_specs=[
            pl.BlockSpec((tm, D), lambda i, j: (i, 0)),
            pl.BlockSpec((1, D), lambda i, j: (0, 0)),
            pl.BlockSpec((D, tn), lambda i, j: (0, j)),
            pl.BlockSpec((tm, P), lambda i, j: (i, 0)),
            pl.BlockSpec((P, tn), lambda i, j: (0, j)),
            pl.BlockSpec((tm, tn), lambda i, j: (i, j)),
        ],
        out_specs=pl.BlockSpec((tm, tn), lambda i, j: (i, j)),
        out_shape=jax.ShapeDtypeStruct((S, D), F32),
        scratch_shapes=[pltpu.VMEM((tm, D), BF16)],
        compiler_params=_params("parallel", "arbitrary"),
        name="ple",
    )(x, g, w_gate, p, w_proj, x)


def _tiles(S):
    return dict(
        proj=dict(tm=min(512, S), tn=512),
        cumsum=dict(tb=min(512, S)),
        fox=dict(tq=min(512, S), tk=min(512, S)),
        sb=dict(tq=min(256, S), tk=128),
        merge=dict(tm=min(1024, S), tn=512),
        out=dict(tm=min(1024, S), tn=512),
        up=dict(tm=min(512, S), tn=512),
        down=dict(tm=min(1024, S), tn=1024, tk=2816),
        ple=dict(tm=min(512, S), tn=512),
        ff_pad=512,
    )


def _pad_cols(a, n):
    return jnp.pad(a, ((0, 0), (0, n - a.shape[1])))


def _layer(x, p, g_mix, w_in, b_f, g_q, g_k, w_bf, w_bs, w_out, g_ffn, w_up, conv_w, conv_b,
           w_down, g_ple, w_ple_gate, w_ple_proj, tiles):
    S, D = x.shape
    d_ff = w_down.shape[0]
    ff_pad = -(-d_ff // tiles["ff_pad"]) * tiles["ff_pad"]
    f0 = 3 * W_ATT

    w_main = jnp.concatenate([w_in[:, :f0], w_in[:, f0 + N_HEADS:]], axis=1).astype(BF16)
    w_f = _pad_cols(w_in[:, f0:f0 + N_HEADS], LANES).astype(BF16)
    b_f_row = _pad_cols(b_f[None, :], LANES)
    w_up_g = _pad_cols(w_up[:, :d_ff], ff_pad).astype(BF16)
    w_up_v = _pad_cols(w_up[:, d_ff:], ff_pad).astype(BF16)
    cw_g = _pad_cols(conv_w[:, :d_ff], ff_pad)
    cw_v = _pad_cols(conv_w[:, d_ff:], ff_pad)
    cb_g = _pad_cols(conv_b[None, :d_ff], ff_pad)
    cb_v = _pad_cols(conv_b[None, d_ff:], ff_pad)
    w_down_p = jnp.pad(w_down, ((0, ff_pad - d_ff), (0, 0))).astype(BF16)

    proj, log_f = _norm_proj(x, g_mix[None, :], w_main, w_f, b_f_row, g_q[None, :], g_k[None, :],
                             **tiles["proj"])
    c = _cumsum(log_f, **tiles["cumsum"])
    tk = tiles["fox"]["tk"]
    c_rows = c[:, :N_HEADS].T.reshape(N_HEADS, S // tk, 1, tk)
    y_a = _fox_attn(proj, c_rows, **tiles["fox"])
    y_b = _sb_attn(proj, **tiles["sb"])
    merged = _merge(y_a, y_b, w_bf.astype(BF16), w_bs.astype(BF16), proj, **tiles["merge"])
    x = _matmul_res(merged, w_out.astype(BF16), x, **tiles["out"])
    gated = _up_glu(x, g_ffn[None, :], w_up_g, w_up_v, cw_g, cw_v, cb_g, cb_v, **tiles["up"])
    x = _down_res(gated, w_down_p, x, **tiles["down"])
    x = _ple(x, g_ple[None, :], w_ple_gate.astype(BF16), p, w_ple_proj.astype(BF16), **tiles["ple"])
    return x


def _forward(x, p, g_mix, w_in, b_f, g_q_fox, g_k_fox, w_branch_fox, w_branch_sb, w_out, g_ffn,
             w_up, conv_w, conv_b, w_down, g_ple, w_ple_gate, w_ple_proj, tiles):
    B, S, D = x.shape
    outs = []
    for b in range(B):
        xb = x[b]
        for i in range(w_in.shape[0]):
            xb = _layer(xb, p[i, b], g_mix[i], w_in[i], b_f[i], g_q_fox[i], g_k_fox[i],
                        w_branch_fox[i], w_branch_sb[i], w_out[i], g_ffn[i], w_up[i], conv_w[i],
                        conv_b[i], w_down[i], g_ple[i], w_ple_gate[i], w_ple_proj[i], tiles)
        outs.append(xb)
    return jnp.stack(outs, axis=0)


def kernel(x, p, g_mix, w_in, b_f, g_q_fox, g_k_fox, w_branch_fox, w_branch_sb, w_out, g_ffn,
           w_up, conv_w, conv_b, w_down, g_ple, w_ple_gate, w_ple_proj):
    return _forward(x, p, g_mix, w_in, b_f, g_q_fox, g_k_fox, w_branch_fox, w_branch_sb, w_out,
                    g_ffn, w_up, conv_w, conv_b, w_down, g_ple, w_ple_gate, w_ple_proj,
                    _tiles(x.shape[1]))
```

```python
import functools

import jax
import jax.numpy as jnp
from jax import lax
from jax.experimental import pallas as pl
from jax.experimental.pallas import tpu as pltpu

F32 = jnp.float32
BF16 = jnp.bfloat16

EPS = 1e-6
HEAD_DIM = 128
N_HEADS = 16
W_ATT = N_HEADS * HEAD_DIM
LANES = 128
SUBLANES = 8
CONV_WIDTH = 3
VMEM_LIMIT_BYTES = 56 * 1024 * 1024
UNDERFLOW = 105.0
LOG2E = 1.4426950408889634
C_SLACK = 2.0
QK_SLACK = 1.02

COL_QA, COL_KA, COL_VA = 0, 16, 32
COL_QB, COL_KB, COL_VB = 48, 64, 80
COL_GA, COL_GB = 96, 128
N_PROJ = 160 * LANES


def _params(*sem):
    return pltpu.CompilerParams(dimension_semantics=sem, vmem_limit_bytes=VMEM_LIMIT_BYTES)


def _log_sigmoid(z):
    return jnp.minimum(z, 0.0) - jnp.log1p(jnp.exp(-jnp.abs(z)))


def _rms_norm_rows(x, g):
    ms = jnp.mean(x * x, axis=-1, keepdims=True)
    return x * lax.rsqrt(ms + EPS) * g


def _dot(a, b):
    return jnp.dot(a, b, preferred_element_type=F32)


def _dot_nt(a, b):
    return lax.dot_general(a, b, (((1,), (1,)), ((), ())), preferred_element_type=F32)


def _norm_proj_kernel(x_ref, g_ref, w_ref, wf_ref, bf_ref, gq_ref, gk_ref,
                      o_ref, lf_ref, h_ref, *, tn):
    j = pl.program_id(1)
    tiles = W_ATT // tn

    @pl.when(j == 0)
    def _():
        h = _rms_norm_rows(x_ref[...], g_ref[...]).astype(BF16)
        h_ref[...] = h
        lf_ref[...] = _log_sigmoid(_dot(h, wf_ref[...]) + bf_ref[...]) * LOG2E

    acc = _dot(h_ref[...], w_ref[...])
    scale = HEAD_DIM ** -0.5 * LOG2E

    def head_norm(g, mult):
        for c in range(tn // HEAD_DIM):
            sl = slice(c * HEAD_DIM, (c + 1) * HEAD_DIM)
            o_ref[:, sl] = (_rms_norm_rows(acc[:, sl], g) * mult).astype(BF16)

    @pl.when(j < tiles)
    def _():
        head_norm(gq_ref[...], scale)

    @pl.when((j >= tiles) & (j < 2 * tiles))
    def _():
        head_norm(gk_ref[...], 1.0)

    @pl.when((j >= 3 * tiles) & (j < 4 * tiles))
    def _():
        o_ref[...] = (acc * scale).astype(BF16)

    @pl.when(((j >= 2 * tiles) & (j < 3 * tiles)) | ((j >= 4 * tiles) & (j < 6 * tiles)))
    def _():
        o_ref[...] = acc.astype(BF16)

    @pl.when(j >= 6 * tiles)
    def _():
        o_ref[...] = jax.nn.sigmoid(acc).astype(BF16)


def _norm_proj(x, g, w_main, w_f, b_f, gq, gk, *, tm, tn):
    S, D = x.shape
    return pl.pallas_call(
        functools.partial(_norm_proj_kernel, tn=tn),
        grid=(S // tm, N_PROJ // tn),
        in_specs=[
            pl.BlockSpec((tm, D), lambda i, j: (i, 0)),
            pl.BlockSpec((1, D), lambda i, j: (0, 0)),
            pl.BlockSpec((D, tn), lambda i, j: (0, j)),
            pl.BlockSpec((D, LANES), lambda i, j: (0, 0)),
            pl.BlockSpec((1, LANES), lambda i, j: (0, 0)),
            pl.BlockSpec((1, HEAD_DIM), lambda i, j: (0, 0)),
            pl.BlockSpec((1, HEAD_DIM), lambda i, j: (0, 0)),
        ],
        out_specs=[
            pl.BlockSpec((tm, tn), lambda i, j: (i, j)),
            pl.BlockSpec((tm, LANES), lambda i, j: (i, 0)),
        ],
        out_shape=[
            jax.ShapeDtypeStruct((S, N_PROJ), BF16),
            jax.ShapeDtypeStruct((S, LANES), F32),
        ],
        scratch_shapes=[pltpu.VMEM((tm, D), BF16)],
        compiler_params=_params("parallel", "arbitrary"),
        name="norm_proj",
    )(x, g, w_main, w_f, b_f, gq, gk)


def _cumsum_kernel(x_ref, o_ref, carry_ref, *, tb):
    @pl.when(pl.program_id(0) == 0)
    def _():
        carry_ref[...] = jnp.zeros_like(carry_ref)

    x = x_ref[...]
    r = lax.broadcasted_iota(jnp.int32, (tb, tb), 0)
    c = lax.broadcasted_iota(jnp.int32, (tb, tb), 1)
    tri = (c <= r).astype(BF16)
    x1 = x.astype(BF16)
    r1 = x - x1.astype(F32)
    x2 = r1.astype(BF16)
    x3 = (r1 - x2.astype(F32)).astype(BF16)
    cs = _dot(tri, x1) + _dot(tri, x2) + _dot(tri, x3) + carry_ref[...]
    o_ref[...] = cs
    carry_ref[...] = cs[tb - 1:tb, :]


def _cumsum(x, *, tb):
    S, W = x.shape
    return pl.pallas_call(
        functools.partial(_cumsum_kernel, tb=tb),
        grid=(S // tb,),
        in_specs=[pl.BlockSpec((tb, W), lambda i: (i, 0))],
        out_specs=pl.BlockSpec((tb, W), lambda i: (i, 0)),
        out_shape=jax.ShapeDtypeStruct((S, W), F32),
        scratch_shapes=[pltpu.VMEM((1, W), F32)],
        compiler_params=_params("arbitrary"),
        name="cumsum_logf",
    )(x)


def _fox_kernel(edge_ref, lim_ref, q_ref, k_ref, v_ref, c_ref, o_ref, m_ref, l_ref, acc_ref,
                *, tq, tk, hp):
    g = pl.program_id(0)
    i = pl.program_id(1)
    nd = tq // tk
    rep = tk // LANES
    m_ref[...] = jnp.full_like(m_ref, -1e30)
    l_ref[...] = jnp.zeros_like(l_ref)
    acc_ref[...] = jnp.zeros_like(acc_ref)
    row = i * tq + lax.broadcasted_iota(jnp.int32, (tq, tk), 0)
    col = lax.broadcasted_iota(jnp.int32, (tq, tk), 1)

    def step(kb, masked):
        start = pl.multiple_of(kb * tk, tk)
        for hh in range(hp):
            sl = slice(hh * HEAD_DIM, (hh + 1) * HEAD_DIM)
            s = _dot_nt(q_ref[:, sl], k_ref[pl.ds(start, tk), sl]) - c_ref[hh, kb]
            if masked:
                s = jnp.where(start + col <= row, s, -jnp.inf)
            m_old = m_ref[hh]
            m_new = jnp.maximum(m_old, jnp.max(s, axis=-1, keepdims=True))
            alpha = jnp.exp2(m_old - m_new)
            p = jnp.exp2(s - jnp.tile(m_new, (1, rep)))
            l_ref[hh] = alpha * l_ref[hh] + jnp.sum(p, axis=-1, keepdims=True)
            acc_ref[hh] = alpha * acc_ref[hh] + _dot(p.astype(BF16), v_ref[pl.ds(start, tk), sl])
            m_ref[hh] = m_new

    for d in range(nd):
        step((i + 1) * nd - 1 - d, True)

    top = i * nd - 1
    lim = lim_ref[0]

    def chunks_needed(h):
        e_hi = edge_ref[h, jnp.maximum(top, 0)]

        def needed(kb):
            return (kb >= 0) & (e_hi - edge_ref[h, jnp.maximum(kb, 0)] > -lim)

        return top - lax.while_loop(needed, lambda kb: kb - 1, top)

    n = chunks_needed(g * hp)
    for hh in range(1, hp):
        n = jnp.maximum(n, chunks_needed(g * hp + hh))

    def body(t, carry):
        step(top - t, False)
        return carry

    lax.fori_loop(0, n, body, 0)
    for hh in range(hp):
        sl = slice(hh * HEAD_DIM, (hh + 1) * HEAD_DIM)
        o_ref[:, sl] = (acc_ref[hh] / l_ref[hh]).astype(BF16)


def _fox_attn(proj, c_rows, c_edges, lim, *, tq, tk, hp):
    S = proj.shape[0]
    nkb = S // tk
    w = hp * HEAD_DIM
    return pl.pallas_call(
        functools.partial(_fox_kernel, tq=tq, tk=tk, hp=hp),
        grid=(N_HEADS // hp, S // tq),
        in_specs=[
            pl.BlockSpec(memory_space=pltpu.SMEM),
            pl.BlockSpec(memory_space=pltpu.SMEM),
            pl.BlockSpec((tq, w), lambda g, i: (i, COL_QA // hp + g)),
            pl.BlockSpec((S, w), lambda g, i: (0, COL_KA // hp + g)),
            pl.BlockSpec((S, w), lambda g, i: (0, COL_VA // hp + g)),
            pl.BlockSpec((hp, nkb, 1, tk), lambda g, i: (g, 0, 0, 0)),
        ],
        out_specs=pl.BlockSpec((tq, w), lambda g, i: (i, g)),
        out_shape=jax.ShapeDtypeStruct((S, W_ATT), BF16),
        scratch_shapes=[
            pltpu.VMEM((hp, tq, LANES), F32),
            pltpu.VMEM((hp, tq, LANES), F32),
            pltpu.VMEM((hp, tq, HEAD_DIM), F32),
        ],
        compiler_params=_params("parallel", "arbitrary"),
        name="fox_attn",
    )(c_edges, lim, proj, proj, proj, c_rows)


def _sb_kernel(q_ref, k_ref, v_ref, o_ref, run_ref, acc_ref, *, tq, tk, hp):
    i = pl.program_id(1)
    nd = tq // tk
    rep = tk // LANES
    run_ref[...] = jnp.zeros_like(run_ref)
    acc_ref[...] = jnp.zeros_like(acc_ref)
    row = i * tq + lax.broadcasted_iota(jnp.int32, (tq, tk), 0)
    col = lax.broadcasted_iota(jnp.int32, (tq, tk), 1)
    jj = lax.broadcasted_iota(jnp.int32, (tk, tk), 0)
    ss = lax.broadcasted_iota(jnp.int32, (tk, tk), 1)
    later = (jj > ss).astype(BF16)

    def cond(carry):
        kb, run_max = carry
        return (kb >= 0) & (run_max > -UNDERFLOW * LOG2E)

    def body(carry):
        kb, _ = carry
        start = pl.multiple_of(kb * tk, tk)
        strict = start + col < row
        run_max = None
        for hh in range(hp):
            sl = slice(hh * HEAD_DIM, (hh + 1) * HEAD_DIM)
            z = _dot_nt(q_ref[:, sl], k_ref[pl.ds(start, tk), sl])
            log_beta = jnp.minimum(z, 0.0) - jnp.log2(1.0 + jnp.exp2(-jnp.abs(z)))
            l1m = jnp.where(strict, log_beta - z, 0.0)
            hi = l1m.astype(BF16)
            lo = (l1m - hi.astype(F32)).astype(BF16)
            tail = _dot(hi, later) + _dot(lo, later)
            run = run_ref[hh]
            a = jnp.where(strict, jnp.exp2(log_beta + tail + jnp.tile(run, (1, rep))), 0.0)
            acc_ref[hh] += _dot(a.astype(BF16), v_ref[pl.ds(start, tk), sl])
            run_new = run + jnp.sum(l1m, axis=-1, keepdims=True)
            run_ref[hh] = run_new
            r = jnp.max(run_new)
            run_max = r if run_max is None else jnp.maximum(run_max, r)
        return kb - 1, run_max

    lax.while_loop(cond, body, ((i + 1) * nd - 1, jnp.float32(0.0)))
    for hh in range(hp):
        sl = slice(hh * HEAD_DIM, (hh + 1) * HEAD_DIM)
        o_ref[:, sl] = acc_ref[hh].astype(BF16)


def _sb_attn(proj, *, tq, tk, hp):
    S = proj.shape[0]
    w = hp * HEAD_DIM
    return pl.pallas_call(
        functools.partial(_sb_kernel, tq=tq, tk=tk, hp=hp),
        grid=(N_HEADS // hp, S // tq),
        in_specs=[
            pl.BlockSpec((tq, w), lambda g, i: (i, COL_QB // hp + g)),
            pl.BlockSpec((S, w), lambda g, i: (0, COL_KB // hp + g)),
            pl.BlockSpec((S, w), lambda g, i: (0, COL_VB // hp + g)),
        ],
        out_specs=pl.BlockSpec((tq, w), lambda g, i: (i, g)),
        out_shape=jax.ShapeDtypeStruct((S, W_ATT), BF16),
        scratch_shapes=[
            pltpu.VMEM((hp, tq, LANES), F32),
            pltpu.VMEM((hp, tq, HEAD_DIM), F32),
        ],
        compiler_params=_params("parallel", "arbitrary"),
        name="sb_attn",
    )(proj, proj, proj)


def _merge_kernel(ya_ref, yb_ref, wa_ref, wb_ref, ga_ref, gb_ref, o_ref):
    a = _dot(ya_ref[...], wa_ref[...])
    b = _dot(yb_ref[...], wb_ref[...])
    o_ref[...] = (ga_ref[...].astype(F32) * a + gb_ref[...].astype(F32) * b).astype(BF16)


def _merge(y_a, y_b, w_a, w_b, proj, *, tm, tn):
    S, K = y_a.shape
    N = w_a.shape[1]
    ga0 = COL_GA * LANES // tn
    gb0 = COL_GB * LANES // tn
    return pl.pallas_call(
        _merge_kernel,
        grid=(S // tm, N // tn),
        in_specs=[
            pl.BlockSpec((tm, K), lambda i, j: (i, 0)),
            pl.BlockSpec((tm, K), lambda i, j: (i, 0)),
            pl.BlockSpec((K, tn), lambda i, j: (0, j)),
            pl.BlockSpec((K, tn), lambda i, j: (0, j)),
            pl.BlockSpec((tm, tn), lambda i, j: (i, ga0 + j)),
            pl.BlockSpec((tm, tn), lambda i, j: (i, gb0 + j)),
        ],
        out_specs=pl.BlockSpec((tm, tn), lambda i, j: (i, j)),
        out_shape=jax.ShapeDtypeStruct((S, N), BF16),
        compiler_params=_params("parallel", "arbitrary"),
        name="merge",
    )(y_a, y_b, w_a, w_b, proj, proj)


def _matmul_res_kernel(a_ref, w_ref, x_ref, o_ref):
    o_ref[...] = x_ref[...] + _dot(a_ref[...], w_ref[...])


def _matmul_res(a, w, x, *, tm, tn):
    S, K = a.shape
    N = w.shape[1]
    return pl.pallas_call(
        _matmul_res_kernel,
        grid=(S // tm, N // tn),
        in_specs=[
            pl.BlockSpec((tm, K), lambda i, j: (i, 0)),
            pl.BlockSpec((K, tn), lambda i, j: (0, j)),
            pl.BlockSpec((tm, tn), lambda i, j: (i, j)),
        ],
        out_specs=pl.BlockSpec((tm, tn), lambda i, j: (i, j)),
        out_shape=jax.ShapeDtypeStruct((S, N), F32),
        compiler_params=_params("parallel", "arbitrary"),
        name="out_res",
    )(a, w, x)


def _up_glu_kernel(x_ref, g_ref, wg_ref, wv_ref, cwg_ref, cwv_ref, cbg_ref, cbv_ref,
                   o_ref, h_ref, ug_ref, uv_ref, carry_g_ref, carry_v_ref, *, tm):
    i = pl.program_id(0)
    j = pl.program_id(1)
    halo = SUBLANES

    @pl.when(j == 0)
    def _():
        h_ref[...] = _rms_norm_rows(x_ref[...], g_ref[...]).astype(BF16)

    h = h_ref[...]

    def conv(w_ref, cw_ref, cb_ref, u_ref, carry_ref):
        u = _dot(h, w_ref[...])

        @pl.when(i == 0)
        def _():
            u_ref[0:halo, :] = jnp.zeros((halo, u.shape[1]), F32)

        @pl.when(i > 0)
        def _():
            u_ref[0:halo, :] = carry_ref[j]

        u_ref[halo:halo + tm, :] = u
        carry_ref[j] = u[tm - halo:tm, :]
        cw = cw_ref[...]
        return (cb_ref[...]
                + cw[0:1, :] * u_ref[halo - 2:halo - 2 + tm, :]
                + cw[1:2, :] * u_ref[halo - 1:halo - 1 + tm, :]
                + cw[2:3, :] * u)

    gate = conv(wg_ref, cwg_ref, cbg_ref, ug_ref, carry_g_ref)
    val = conv(wv_ref, cwv_ref, cbv_ref, uv_ref, carry_v_ref)
    o_ref[...] = (gate * jax.nn.sigmoid(gate) * val).astype(BF16)


def _up_glu(x, g, w_g, w_v, cw_g, cw_v, cb_g, cb_v, *, tm, tn):
    S, D = x.shape
    NF = w_g.shape[1]
    nj = NF // tn
    wspec = pl.BlockSpec((D, tn), lambda i, j: (0, j))
    cwspec = pl.BlockSpec((CONV_WIDTH, tn), lambda i, j: (0, j))
    cbspec = pl.BlockSpec((1, tn), lambda i, j: (0, j))
    return pl.pallas_call(
        functools.partial(_up_glu_kernel, tm=tm),
        grid=(S // tm, nj),
        in_specs=[
            pl.BlockSpec((tm, D), lambda i, j: (i, 0)),
            pl.BlockSpec((1, D), lambda i, j: (0, 0)),
            wspec, wspec, cwspec, cwspec, cbspec, cbspec,
        ],
        out_specs=pl.BlockSpec((tm, tn), lambda i, j: (i, j)),
        out_shape=jax.ShapeDtypeStruct((S, NF), BF16),
        scratch_shapes=[
            pltpu.VMEM((tm, D), BF16),
            pltpu.VMEM((tm + SUBLANES, tn), F32),
            pltpu.VMEM((tm + SUBLANES, tn), F32),
            pltpu.VMEM((nj, SUBLANES, tn), F32),
            pltpu.VMEM((nj, SUBLANES, tn), F32),
        ],
        compiler_params=_params("arbitrary", "arbitrary"),
        name="up_glu",
    )(x, g, w_g, w_v, cw_g, cw_v, cb_g, cb_v)


def _down_res_kernel(a_ref, w_ref, x_ref, o_ref, acc_ref):
    k = pl.program_id(2)

    @pl.when(k == 0)
    def _():
        acc_ref[...] = x_ref[...]

    acc_ref[...] += _dot(a_ref[...], w_ref[...])

    @pl.when(k == pl.num_programs(2) - 1)
    def _():
        o_ref[...] = acc_ref[...]


def _down_res(a, w, x, *, tm, tn, tk):
    S, K = a.shape
    N = w.shape[1]
    return pl.pallas_call(
        _down_res_kernel,
        grid=(S // tm, N // tn, K // tk),
        in_specs=[
            pl.BlockSpec((tm, tk), lambda i, j, k: (i, k)),
            pl.BlockSpec((tk, tn), lambda i, j, k: (k, j)),
            pl.BlockSpec((tm, tn), lambda i, j, k: (i, j)),
        ],
        out_specs=pl.BlockSpec((tm, tn), lambda i, j, k: (i, j)),
        out_shape=jax.ShapeDtypeStruct((S, N), F32),
        scratch_shapes=[pltpu.VMEM((tm, tn), F32)],
        compiler_params=_params("parallel", "parallel", "arbitrary"),
        name="down_res",
    )(a, w, x)


def _ple_kernel(x_ref, g_ref, wg_ref, p_ref, wp_ref, xr_ref, o_ref, h_ref):
    @pl.when(pl.program_id(1) == 0)
    def _():
        h_ref[...] = _rms_norm_rows(x_ref[...], g_ref[...]).astype(BF16)

    gate = jax.nn.sigmoid(_dot(h_ref[...], wg_ref[...]))
    emb = _dot(p_ref[...].astype(BF16), wp_ref[...])
    o_ref[...] = xr_ref[...] + gate * emb


def _ple(x, g, w_gate, p, w_proj, *, tm, tn):
    S, D = x.shape
    P = p.shape[1]
    return pl.pallas_call(
        _ple_kernel,
        grid=(S // tm, D // tn),
        in_specs=[
            pl.BlockSpec((tm, D), lambda i, j: (i, 0)),
            pl.BlockSpec((1, D), lambda i, j: (0, 0)),
            pl.BlockSpec((D, tn), lambda i, j: (0, j)),
            pl.BlockSpec((tm, P), lambda i, j: (i, 0)),
            pl.BlockSpec((P, tn), lambda i, j: (0, j)),
            pl.BlockSpec((tm, tn), lambda i, j: (i, j)),
        ],
        out_specs=pl.BlockSpec((tm, tn), lambda i, j: (i, j)),
        out_shape=jax.ShapeDtypeStruct((S, D), F32),
        scratch_shapes=[pltpu.VMEM((tm, D), BF16)],
        compiler_params=_params("parallel", "arbitrary"),
        name="ple",
    )(x, g, w_gate, p, w_proj, x)


def _tiles(S):
    return dict(
        proj=dict(tm=min(512, S), tn=512),
        cumsum=dict(tb=min(512, S)),
        fox=dict(tq=min(512, S), tk=min(512, S), hp=2),
        sb=dict(tq=min(256, S), tk=min(256, S), hp=2),
        merge=dict(tm=min(1024, S), tn=512),
        out=dict(tm=min(1024, S), tn=512),
        up=dict(tm=min(512, S), tn=512),
        down=dict(tm=min(1024, S), tn=1024, tk=2816),
        ple=dict(tm=min(512, S), tn=512),
        ff_pad=512,
    )


def _pad_cols(a, n):
    return jnp.pad(a, ((0, 0), (0, n - a.shape[1])))


def _layer(x, p, g_mix, w_in, b_f, g_q, g_k, w_bf, w_bs, w_out, g_ffn, w_up, conv_w, conv_b,
           w_down, g_ple, w_ple_gate, w_ple_proj, tiles):
    S, D = x.shape
    d_ff = w_down.shape[0]
    ff_pad = -(-d_ff // tiles["ff_pad"]) * tiles["ff_pad"]
    f0 = 3 * W_ATT

    w_main = jnp.concatenate([w_in[:, :f0], w_in[:, f0 + N_HEADS:]], axis=1).astype(BF16)
    w_f = _pad_cols(w_in[:, f0:f0 + N_HEADS], LANES).astype(BF16)
    b_f_row = _pad_cols(b_f[None, :], LANES)
    w_up_g = _pad_cols(w_up[:, :d_ff], ff_pad).astype(BF16)
    w_up_v = _pad_cols(w_up[:, d_ff:], ff_pad).astype(BF16)
    cw_g = _pad_cols(conv_w[:, :d_ff], ff_pad)
    cw_v = _pad_cols(conv_w[:, d_ff:], ff_pad)
    cb_g = _pad_cols(conv_b[None, :d_ff], ff_pad)
    cb_v = _pad_cols(conv_b[None, d_ff:], ff_pad)
    w_down_p = jnp.pad(w_down, ((0, ff_pad - d_ff), (0, 0))).astype(BF16)

    proj, log_f = _norm_proj(x, g_mix[None, :], w_main, w_f, b_f_row, g_q[None, :], g_k[None, :],
                             **tiles["proj"])
    c = _cumsum(log_f, **tiles["cumsum"])
    tk = tiles["fox"]["tk"]
    c_rows = c[:, :N_HEADS].T.reshape(N_HEADS, S // tk, 1, tk)
    c_edges = c_rows[:, :, 0, tk - 1]
    qk_max = HEAD_DIM ** 0.5 * LOG2E * QK_SLACK * jnp.max(jnp.abs(g_q)) * jnp.max(jnp.abs(g_k))
    lim = ((UNDERFLOW + C_SLACK) * LOG2E + 2.0 * qk_max).reshape(1)
    y_a = _fox_attn(proj, c_rows, c_edges, lim, **tiles["fox"])
    y_b = _sb_attn(proj, **tiles["sb"])
    merged = _merge(y_a, y_b, w_bf.astype(BF16), w_bs.astype(BF16), proj, **tiles["merge"])
    x = _matmul_res(merged, w_out.astype(BF16), x, **tiles["out"])
    gated = _up_glu(x, g_ffn[None, :], w_up_g, w_up_v, cw_g, cw_v, cb_g, cb_v, **tiles["up"])
    x = _down_res(gated, w_down_p, x, **tiles["down"])
    x = _ple(x, g_ple[None, :], w_ple_gate.astype(BF16), p, w_ple_proj.astype(BF16), **tiles["ple"])
    return x


def _forward(x, p, g_mix, w_in, b_f, g_q_fox, g_k_fox, w_branch_fox, w_branch_sb, w_out, g_ffn,
             w_up, conv_w, conv_b, w_down, g_ple, w_ple_gate, w_ple_proj, tiles):
    B, S, D = x.shape
    outs = []
    for b in range(B):
        xb = x[b]
        for i in range(w_in.shape[0]):
            xb = _layer(xb, p[i, b], g_mix[i], w_in[i], b_f[i], g_q_fox[i], g_k_fox[i],
                        w_branch_fox[i], w_branch_sb[i], w_out[i], g_ffn[i], w_up[i], conv_w[i],
                        conv_b[i], w_down[i], g_ple[i], w_ple_gate[i], w_ple_proj[i], tiles)
        outs.append(xb)
    return jnp.stack(outs, axis=0)


def kernel(x, p, g_mix, w_in, b_f, g_q_fox, g_k_fox, w_branch_fox, w_branch_sb, w_out, g_ffn,
           w_up, conv_w, conv_b, w_down, g_ple, w_ple_gate, w_ple_proj):
    return _forward(x, p, g_mix, w_in, b_f, g_q_fox, g_k_fox, w_branch_fox, w_branch_sb, w_out,
                    g_ffn, w_up, conv_w, conv_b, w_down, g_ple, w_ple_gate, w_ple_proj,
                    _tiles(x.shape[1]))
```

```python
import functools

import jax
import jax.numpy as jnp
from jax import lax
from jax.experimental import pallas as pl
from jax.experimental.pallas import tpu as pltpu

F32 = jnp.float32
BF16 = jnp.bfloat16

EPS = 1e-6
HEAD_DIM = 128
N_HEADS = 16
W_ATT = N_HEADS * HEAD_DIM
LANES = 128
SUBLANES = 8
MXU_COLS = 256
CONV_WIDTH = 3
VMEM_LIMIT_BYTES = 56 * 1024 * 1024
UNDERFLOW = 105.0
LOG2E = 1.4426950408889634
C_SLACK = 2.0
QK_SLACK = 1.02

COL_QA, COL_KA, COL_VA = 0, 16, 32
COL_QB, COL_KB, COL_VB, COL_GA, COL_GB = 0, 16, 32, 48, 80


def _params(*sem):
    return pltpu.CompilerParams(dimension_semantics=sem, vmem_limit_bytes=VMEM_LIMIT_BYTES)


def _log_sigmoid(z):
    return jnp.minimum(z, 0.0) - jnp.log1p(jnp.exp(-jnp.abs(z)))


def _rms_norm_rows(x, g):
    ms = jnp.mean(x * x, axis=-1, keepdims=True)
    return x * lax.rsqrt(ms + EPS) * g


def _dot(a, b):
    return jnp.dot(a, b, preferred_element_type=F32)


def _dot_nt(a, b):
    return lax.dot_general(a, b, (((1,), (1,)), ((), ())), preferred_element_type=F32)


def _col_blocks(width):
    return [slice(s, s + MXU_COLS) for s in range(0, width, MXU_COLS)]


def _norm_kernel(x_ref, g_ref, o_ref):
    o_ref[...] = _rms_norm_rows(x_ref[...], g_ref[...]).astype(BF16)


def _norm_gate_kernel(x_ref, g_ref, wf_ref, bf_ref, o_ref, lf_ref):
    h = _rms_norm_rows(x_ref[...], g_ref[...]).astype(BF16)
    o_ref[...] = h
    lf_ref[...] = _log_sigmoid(_dot(h, wf_ref[...]) + bf_ref[...]) * LOG2E


def _norm(x, g, *, tr, w_f=None, b_f=None):
    S, D = x.shape
    row = pl.BlockSpec((tr, D), lambda i: (i, 0))
    gain = pl.BlockSpec((1, D), lambda i: (0, 0))
    if w_f is None:
        return pl.pallas_call(
            _norm_kernel, grid=(S // tr,), in_specs=[row, gain], out_specs=row,
            out_shape=jax.ShapeDtypeStruct((S, D), BF16),
            compiler_params=_params("parallel"), name="norm",
        )(x, g)
    return pl.pallas_call(
        _norm_gate_kernel, grid=(S // tr,),
        in_specs=[row, gain, pl.BlockSpec((D, LANES), lambda i: (0, 0)),
                  pl.BlockSpec((1, LANES), lambda i: (0, 0))],
        out_specs=[row, pl.BlockSpec((tr, LANES), lambda i: (i, 0))],
        out_shape=[jax.ShapeDtypeStruct((S, D), BF16), jax.ShapeDtypeStruct((S, LANES), F32)],
        compiler_params=_params("parallel"), name="norm_gate",
    )(x, g, w_f, b_f)


def _proj_sweep(h_ref, w_ref, o_ref, epilogue):
    h = h_ref[...]
    for sl in _col_blocks(w_ref.shape[1]):
        o_ref[:, sl] = epilogue(_dot(h, w_ref[:, sl])).astype(BF16)


def _proj_fox_kernel(h_ref, w_ref, gq_ref, gk_ref, o_ref, *, tn):
    j = pl.program_id(1)
    tiles = W_ATT // tn

    def head_norm(g_ref, mult):
        def epilogue(acc):
            heads = [_rms_norm_rows(acc[:, c:c + HEAD_DIM], g_ref[...]) * mult
                     for c in range(0, acc.shape[1], HEAD_DIM)]
            return jnp.concatenate(heads, axis=1)
        return epilogue

    @pl.when(j < tiles)
    def _():
        _proj_sweep(h_ref, w_ref, o_ref, head_norm(gq_ref, HEAD_DIM ** -0.5 * LOG2E))

    @pl.when((j >= tiles) & (j < 2 * tiles))
    def _():
        _proj_sweep(h_ref, w_ref, o_ref, head_norm(gk_ref, 1.0))

    @pl.when(j >= 2 * tiles)
    def _():
        _proj_sweep(h_ref, w_ref, o_ref, lambda acc: acc)


def _proj_sb_kernel(h_ref, w_ref, o_ref, *, tn):
    j = pl.program_id(1)
    tiles = W_ATT // tn

    @pl.when(j < tiles)
    def _():
        _proj_sweep(h_ref, w_ref, o_ref, lambda acc: acc * (HEAD_DIM ** -0.5 * LOG2E))

    @pl.when((j >= tiles) & (j < 3 * tiles))
    def _():
        _proj_sweep(h_ref, w_ref, o_ref, lambda acc: acc)

    @pl.when(j >= 3 * tiles)
    def _():
        _proj_sweep(h_ref, w_ref, o_ref, jax.nn.sigmoid)


def _proj(kernel, name, h, w, gains, *, tm, tn):
    S, D = h.shape
    N = w.shape[1]
    return pl.pallas_call(
        functools.partial(kernel, tn=tn),
        grid=(S // tm, N // tn),
        in_specs=[pl.BlockSpec((tm, D), lambda i, j: (i, 0)),
                  pl.BlockSpec((D, tn), lambda i, j: (0, j))]
                 + [pl.BlockSpec((1, HEAD_DIM), lambda i, j: (0, 0)) for _ in gains],
        out_specs=pl.BlockSpec((tm, tn), lambda i, j: (i, j)),
        out_shape=jax.ShapeDtypeStruct((S, N), BF16),
        compiler_params=_params("parallel", "arbitrary"),
        name=name,
    )(h, w, *gains)


def _cumsum_kernel(x_ref, o_ref, carry_ref, *, tb):
    @pl.when(pl.program_id(0) == 0)
    def _():
        carry_ref[...] = jnp.zeros_like(carry_ref)

    x = x_ref[...]
    r = lax.broadcasted_iota(jnp.int32, (tb, tb), 0)
    c = lax.broadcasted_iota(jnp.int32, (tb, tb), 1)
    tri = (c <= r).astype(BF16)
    x1 = x.astype(BF16)
    r1 = x - x1.astype(F32)
    x2 = r1.astype(BF16)
    x3 = (r1 - x2.astype(F32)).astype(BF16)
    cs = _dot(tri, x1) + _dot(tri, x2) + _dot(tri, x3) + carry_ref[...]
    o_ref[...] = cs
    carry_ref[...] = cs[tb - 1:tb, :]


def _cumsum(x, *, tb):
    S, W = x.shape
    return pl.pallas_call(
        functools.partial(_cumsum_kernel, tb=tb),
        grid=(S // tb,),
        in_specs=[pl.BlockSpec((tb, W), lambda i: (i, 0))],
        out_specs=pl.BlockSpec((tb, W), lambda i: (i, 0)),
        out_shape=jax.ShapeDtypeStruct((S, W), F32),
        scratch_shapes=[pltpu.VMEM((1, W), F32)],
        compiler_params=_params("arbitrary"),
        name="cumsum_logf",
    )(x)


def _fox_kernel(edge_ref, lim_ref, q_ref, k_ref, v_ref, c_ref, o_ref, m_ref, l_ref, acc_ref,
                *, tq, tk, hp):
    g = pl.program_id(0)
    i = pl.program_id(1)
    nd = tq // tk
    rep = tk // LANES
    m_ref[...] = jnp.full_like(m_ref, -1e30)
    l_ref[...] = jnp.zeros_like(l_ref)
    acc_ref[...] = jnp.zeros_like(acc_ref)
    row = i * tq + lax.broadcasted_iota(jnp.int32, (tq, tk), 0)
    col = lax.broadcasted_iota(jnp.int32, (tq, tk), 1)

    def step(kb, masked):
        start = pl.multiple_of(kb * tk, tk)
        for hh in range(hp):
            sl = slice(hh * HEAD_DIM, (hh + 1) * HEAD_DIM)
            s = _dot_nt(q_ref[:, sl], k_ref[pl.ds(start, tk), sl]) - c_ref[hh, kb]
            if masked:
                s = jnp.where(start + col <= row, s, -jnp.inf)
            m_old = m_ref[hh]
            m_new = jnp.maximum(m_old, jnp.max(s, axis=-1, keepdims=True))
            alpha = jnp.exp2(m_old - m_new)
            p = jnp.exp2(s - jnp.tile(m_new, (1, rep)))
            l_ref[hh] = alpha * l_ref[hh] + jnp.sum(p, axis=-1, keepdims=True)
            acc_ref[hh] = alpha * acc_ref[hh] + _dot(p.astype(BF16), v_ref[pl.ds(start, tk), sl])
            m_ref[hh] = m_new

    for d in range(nd):
        step((i + 1) * nd - 1 - d, True)

    top = i * nd - 1
    lim = lim_ref[0]

    def chunks_needed(h):
        e_hi = edge_ref[h, jnp.maximum(top, 0)]

        def needed(kb):
            return (kb >= 0) & (e_hi - edge_ref[h, jnp.maximum(kb, 0)] > -lim)

        return top - lax.while_loop(needed, lambda kb: kb - 1, top)

    n = chunks_needed(g * hp)
    for hh in range(1, hp):
        n = jnp.maximum(n, chunks_needed(g * hp + hh))

    def body(t, carry):
        step(top - t, False)
        return carry

    lax.fori_loop(0, n, body, 0)
    for hh in range(hp):
        sl = slice(hh * HEAD_DIM, (hh + 1) * HEAD_DIM)
        o_ref[:, sl] = (acc_ref[hh] / l_ref[hh]).astype(BF16)


def _fox_attn(proj, c_rows, c_edges, lim, *, tq, tk, hp):
    S = proj.shape[0]
    nkb = S // tk
    w = hp * HEAD_DIM
    return pl.pallas_call(
        functools.partial(_fox_kernel, tq=tq, tk=tk, hp=hp),
        grid=(N_HEADS // hp, S // tq),
        in_specs=[
            pl.BlockSpec(memory_space=pltpu.SMEM),
            pl.BlockSpec(memory_space=pltpu.SMEM),
            pl.BlockSpec((tq, w), lambda g, i: (i, COL_QA // hp + g)),
            pl.BlockSpec((S, w), lambda g, i: (0, COL_KA // hp + g)),
            pl.BlockSpec((S, w), lambda g, i: (0, COL_VA // hp + g)),
            pl.BlockSpec((hp, nkb, 1, tk), lambda g, i: (g, 0, 0, 0)),
        ],
        out_specs=pl.BlockSpec((tq, w), lambda g, i: (i, g)),
        out_shape=jax.ShapeDtypeStruct((S, W_ATT), BF16),
        scratch_shapes=[
            pltpu.VMEM((hp, tq, LANES), F32),
            pltpu.VMEM((hp, tq, LANES), F32),
            pltpu.VMEM((hp, tq, HEAD_DIM), F32),
        ],
        compiler_params=_params("parallel", "arbitrary"),
        name="fox_attn",
    )(c_edges, lim, proj, proj, proj, c_rows)


def _sb_kernel(q_ref, k_ref, v_ref, o_ref, run_ref, acc_ref, *, tq, tk, hp):
    i = pl.program_id(1)
    nd = tq // tk
    rep = tk // LANES
    run_ref[...] = jnp.zeros_like(run_ref)
    acc_ref[...] = jnp.zeros_like(acc_ref)
    row = i * tq + lax.broadcasted_iota(jnp.int32, (tq, tk), 0)
    col = lax.broadcasted_iota(jnp.int32, (tq, tk), 1)
    jj = lax.broadcasted_iota(jnp.int32, (tk, tk), 0)
    ss = lax.broadcasted_iota(jnp.int32, (tk, tk), 1)
    later = (jj > ss).astype(BF16)

    def cond(carry):
        kb, run_max = carry
        return (kb >= 0) & (run_max > -UNDERFLOW * LOG2E)

    def body(carry):
        kb, _ = carry
        start = pl.multiple_of(kb * tk, tk)
        strict = start + col < row
        run_max = None
        for hh in range(hp):
            sl = slice(hh * HEAD_DIM, (hh + 1) * HEAD_DIM)
            z = _dot_nt(q_ref[:, sl], k_ref[pl.ds(start, tk), sl])
            log_beta = jnp.minimum(z, 0.0) - jnp.log2(1.0 + jnp.exp2(-jnp.abs(z)))
            l1m = jnp.where(strict, log_beta - z, 0.0)
            hi = l1m.astype(BF16)
            lo = (l1m - hi.astype(F32)).astype(BF16)
            tail = _dot(hi, later) + _dot(lo, later)
            run = run_ref[hh]
            a = jnp.where(strict, jnp.exp2(log_beta + tail + jnp.tile(run, (1, rep))), 0.0)
            acc_ref[hh] += _dot(a.astype(BF16), v_ref[pl.ds(start, tk), sl])
            run_new = run + jnp.sum(l1m, axis=-1, keepdims=True)
            run_ref[hh] = run_new
            r = jnp.max(run_new)
            run_max = r if run_max is None else jnp.maximum(run_max, r)
        return kb - 1, run_max

    lax.while_loop(cond, body, ((i + 1) * nd - 1, jnp.float32(0.0)))
    for hh in range(hp):
        sl = slice(hh * HEAD_DIM, (hh + 1) * HEAD_DIM)
        o_ref[:, sl] = acc_ref[hh].astype(BF16)


def _sb_attn(proj, *, tq, tk, hp):
    S = proj.shape[0]
    w = hp * HEAD_DIM
    return pl.pallas_call(
        functools.partial(_sb_kernel, tq=tq, tk=tk, hp=hp),
        grid=(N_HEADS // hp, S // tq),
        in_specs=[
            pl.BlockSpec((tq, w), lambda g, i: (i, COL_QB // hp + g)),
            pl.BlockSpec((S, w), lambda g, i: (0, COL_KB // hp + g)),
            pl.BlockSpec((S, w), lambda g, i: (0, COL_VB // hp + g)),
        ],
        out_specs=pl.BlockSpec((tq, w), lambda g, i: (i, g)),
        out_shape=jax.ShapeDtypeStruct((S, W_ATT), BF16),
        scratch_shapes=[
            pltpu.VMEM((hp, tq, LANES), F32),
            pltpu.VMEM((hp, tq, HEAD_DIM), F32),
        ],
        compiler_params=_params("parallel", "arbitrary"),
        name="sb_attn",
    )(proj, proj, proj)


def _merge_kernel(ya_ref, yb_ref, wa_ref, wb_ref, ga_ref, gb_ref, o_ref):
    ya = ya_ref[...]
    yb = yb_ref[...]
    for sl in _col_blocks(o_ref.shape[1]):
        a = _dot(ya, wa_ref[:, sl])
        b = _dot(yb, wb_ref[:, sl])
        o_ref[:, sl] = (ga_ref[:, sl].astype(F32) * a + gb_ref[:, sl].astype(F32) * b).astype(BF16)


def _merge(y_a, y_b, w_a, w_b, proj, *, tm, tn):
    S, K = y_a.shape
    N = w_a.shape[1]
    ga0 = COL_GA * LANES // tn
    gb0 = COL_GB * LANES // tn
    return pl.pallas_call(
        _merge_kernel,
        grid=(S // tm, N // tn),
        in_specs=[
            pl.BlockSpec((tm, K), lambda i, j: (i, 0)),
            pl.BlockSpec((tm, K), lambda i, j: (i, 0)),
            pl.BlockSpec((K, tn), lambda i, j: (0, j)),
            pl.BlockSpec((K, tn), lambda i, j: (0, j)),
            pl.BlockSpec((tm, tn), lambda i, j: (i, ga0 + j)),
            pl.BlockSpec((tm, tn), lambda i, j: (i, gb0 + j)),
        ],
        out_specs=pl.BlockSpec((tm, tn), lambda i, j: (i, j)),
        out_shape=jax.ShapeDtypeStruct((S, N), BF16),
        compiler_params=_params("parallel", "arbitrary"),
        name="merge",
    )(y_a, y_b, w_a, w_b, proj, proj)


def _matmul_res_kernel(a_ref, w_ref, x_ref, o_ref):
    a = a_ref[...]
    for sl in _col_blocks(o_ref.shape[1]):
        o_ref[:, sl] = x_ref[:, sl] + _dot(a, w_ref[:, sl])


def _matmul_res(a, w, x, *, tm, tn):
    S, K = a.shape
    N = w.shape[1]
    return pl.pallas_call(
        _matmul_res_kernel,
        grid=(S // tm, N // tn),
        in_specs=[
            pl.BlockSpec((tm, K), lambda i, j: (i, 0)),
            pl.BlockSpec((K, tn), lambda i, j: (0, j)),
            pl.BlockSpec((tm, tn), lambda i, j: (i, j)),
        ],
        out_specs=pl.BlockSpec((tm, tn), lambda i, j: (i, j)),
        out_shape=jax.ShapeDtypeStruct((S, N), F32),
        compiler_params=_params("parallel", "arbitrary"),
        name="out_res",
    )(a, w, x)


def _up_glu_kernel(h_ref, wg_ref, wv_ref, cwg_ref, cwv_ref, cbg_ref, cbv_ref,
                   o_ref, ug_ref, uv_ref, carry_g_ref, carry_v_ref, *, tm, ts):
    i = pl.program_id(0)
    j = pl.program_id(1)
    halo = SUBLANES
    first = i == 0

    def conv(r0, w_ref, cw_ref, cb_ref, u_ref, carry_ref):
        u = _dot(h_ref[r0:r0 + ts, :], w_ref[...])
        if r0 == 0:
            u_ref[0:halo, :] = jnp.where(first, 0.0, carry_ref[j])
        u_ref[halo + r0:halo + r0 + ts, :] = u
        if r0 + ts == tm:
            carry_ref[j] = u[ts - halo:ts, :]
        cw = cw_ref[...]
        return (cb_ref[...]
                + cw[0:1, :] * u_ref[halo - 2 + r0:halo - 2 + r0 + ts, :]
                + cw[1:2, :] * u_ref[halo - 1 + r0:halo - 1 + r0 + ts, :]
                + cw[2:3, :] * u)

    for r0 in range(0, tm, ts):
        gate = conv(r0, wg_ref, cwg_ref, cbg_ref, ug_ref, carry_g_ref)
        val = conv(r0, wv_ref, cwv_ref, cbv_ref, uv_ref, carry_v_ref)
        o_ref[r0:r0 + ts, :] = (gate * jax.nn.sigmoid(gate) * val).astype(BF16)


def _up_glu(h, w_up, conv_w, conv_b, *, tm, tn, ts):
    S, D = h.shape
    d_ff = w_up.shape[1] // 2
    nj = d_ff // tn
    g_map = lambda i, j: (0, j)
    v_map = lambda i, j: (0, nj + j)
    return pl.pallas_call(
        functools.partial(_up_glu_kernel, tm=tm, ts=ts),
        grid=(S // tm, nj),
        in_specs=[
            pl.BlockSpec((tm, D), lambda i, j: (i, 0)),
            pl.BlockSpec((D, tn), g_map), pl.BlockSpec((D, tn), v_map),
            pl.BlockSpec((CONV_WIDTH, tn), g_map), pl.BlockSpec((CONV_WIDTH, tn), v_map),
            pl.BlockSpec((1, tn), g_map), pl.BlockSpec((1, tn), v_map),
        ],
        out_specs=pl.BlockSpec((tm, tn), lambda i, j: (i, j)),
        out_shape=jax.ShapeDtypeStruct((S, d_ff), BF16),
        scratch_shapes=[
            pltpu.VMEM((tm + SUBLANES, tn), F32),
            pltpu.VMEM((tm + SUBLANES, tn), F32),
            pltpu.VMEM((nj, SUBLANES, tn), F32),
            pltpu.VMEM((nj, SUBLANES, tn), F32),
        ],
        compiler_params=_params("arbitrary", "arbitrary"),
        name="up_glu",
    )(h, w_up, w_up, conv_w, conv_w, conv_b, conv_b)


def _down_res_kernel(a_ref, w_ref, x_ref, o_ref, acc_ref):
    k = pl.program_id(2)

    @pl.when(k == 0)
    def _():
        acc_ref[...] = x_ref[...]

    acc_ref[...] += _dot(a_ref[...], w_ref[...])

    @pl.when(k == pl.num_programs(2) - 1)
    def _():
        o_ref[...] = acc_ref[...]


def _down_res(a, w, x, *, tm, tn, tk):
    S, K = a.shape
    N = w.shape[1]
    return pl.pallas_call(
        _down_res_kernel,
        grid=(S // tm, N // tn, K // tk),
        in_specs=[
            pl.BlockSpec((tm, tk), lambda i, j, k: (i, k)),
            pl.BlockSpec((tk, tn), lambda i, j, k: (k, j)),
            pl.BlockSpec((tm, tn), lambda i, j, k: (i, j)),
        ],
        out_specs=pl.BlockSpec((tm, tn), lambda i, j, k: (i, j)),
        out_shape=jax.ShapeDtypeStruct((S, N), F32),
        scratch_shapes=[pltpu.VMEM((tm, tn), F32)],
        compiler_params=_params("parallel", "parallel", "arbitrary"),
        name="down_res",
    )(a, w, x)


def _ple_kernel(h_ref, wg_ref, p_ref, wp_ref, x_ref, o_ref):
    h = h_ref[...]
    p = p_ref[...].astype(BF16)
    for sl in _col_blocks(o_ref.shape[1]):
        gate = jax.nn.sigmoid(_dot(h, wg_ref[:, sl]))
        o_ref[:, sl] = x_ref[:, sl] + gate * _dot(p, wp_ref[:, sl])


def _ple(h, w_gate, p, w_proj, x, *, tm, tn):
    S, D = h.shape
    P = p.shape[1]
    return pl.pallas_call(
        _ple_kernel,
        grid=(S // tm, D // tn),
        in_specs=[
            pl.BlockSpec((tm, D), lambda i, j: (i, 0)),
            pl.BlockSpec((D, tn), lambda i, j: (0, j)),
            pl.BlockSpec((tm, P), lambda i, j: (i, 0)),
            pl.BlockSpec((P, tn), lambda i, j: (0, j)),
            pl.BlockSpec((tm, tn), lambda i, j: (i, j)),
        ],
        out_specs=pl.BlockSpec((tm, tn), lambda i, j: (i, j)),
        out_shape=jax.ShapeDtypeStruct((S, D), F32),
        compiler_params=_params("parallel", "arbitrary"),
        name="ple",
    )(h, w_gate, p, w_proj, x)


def _tiles(S):
    return dict(
        norm=dict(tr=min(256, S)),
        proj=dict(tm=min(1024, S), tn=1024),
        cumsum=dict(tb=min(512, S)),
        fox=dict(tq=min(512, S), tk=min(512, S), hp=2),
        sb=dict(tq=min(256, S), tk=min(256, S), hp=2),
        merge=dict(tm=min(1024, S), tn=512),
        out=dict(tm=min(1024, S), tn=512),
        up=dict(tm=min(1024, S), tn=256, ts=min(1024, S)),
        down=dict(tm=min(1024, S), tn=512, tk=5504),
        ple=dict(tm=min(1024, S), tn=512),
    )


def _layer(x, p, g_mix, w_in, b_f, g_q, g_k, w_bf, w_bs, w_out, g_ffn, w_up, conv_w, conv_b,
           w_down, g_ple, w_ple_gate, w_ple_proj, tiles):
    S, D = x.shape
    f0 = 3 * W_ATT

    w_fox = w_in[:, :f0].astype(BF16)
    w_sb = w_in[:, f0 + N_HEADS:].astype(BF16)
    w_f = jnp.pad(w_in[:, f0:f0 + N_HEADS], ((0, 0), (0, LANES - N_HEADS))).astype(BF16)
    b_f_row = jnp.pad(b_f[None, :], ((0, 0), (0, LANES - N_HEADS)))

    h, log_f = _norm(x, g_mix[None, :], w_f=w_f, b_f=b_f_row, **tiles["norm"])
    proj_a = _proj(_proj_fox_kernel, "proj_fox", h, w_fox, (g_q[None, :], g_k[None, :]),
                   **tiles["proj"])
    proj_b = _proj(_proj_sb_kernel, "proj_sb", h, w_sb, (), **tiles["proj"])
    c = _cumsum(log_f, **tiles["cumsum"])
    tk = tiles["fox"]["tk"]
    c_rows = c[:, :N_HEADS].T.reshape(N_HEADS, S // tk, 1, tk)
    c_edges = c_rows[:, :, 0, tk - 1]
    qk_max = HEAD_DIM ** 0.5 * LOG2E * QK_SLACK * jnp.max(jnp.abs(g_q)) * jnp.max(jnp.abs(g_k))
    lim = ((UNDERFLOW + C_SLACK) * LOG2E + 2.0 * qk_max).reshape(1)
    y_a = _fox_attn(proj_a, c_rows, c_edges, lim, **tiles["fox"])
    y_b = _sb_attn(proj_b, **tiles["sb"])
    merged = _merge(y_a, y_b, w_bf.astype(BF16), w_bs.astype(BF16), proj_b, **tiles["merge"])
    x = _matmul_res(merged, w_out.astype(BF16), x, **tiles["out"])
    h = _norm(x, g_ffn[None, :], **tiles["norm"])
    gated = _up_glu(h, w_up.astype(BF16), conv_w, conv_b[None, :], **tiles["up"])
    x = _down_res(gated, w_down.astype(BF16), x, **tiles["down"])
    h = _norm(x, g_ple[None, :], **tiles["norm"])
    x = _ple(h, w_ple_gate.astype(BF16), p, w_ple_proj.astype(BF16), x, **tiles["ple"])
    return x


def _forward(x, p, g_mix, w_in, b_f, g_q_fox, g_k_fox, w_branch_fox, w_branch_sb, w_out, g_ffn,
             w_up, conv_w, conv_b, w_down, g_ple, w_ple_gate, w_ple_proj, tiles):
    B, S, D = x.shape
    outs = []
    for b in range(B):
        xb = x[b]
        for i in range(w_in.shape[0]):
            xb = _layer(xb, p[i, b], g_mix[i], w_in[i], b_f[i], g_q_fox[i], g_k_fox[i],
                        w_branch_fox[i], w_branch_sb[i], w_out[i], g_ffn[i], w_up[i], conv_w[i],
                        conv_b[i], w_down[i], g_ple[i], w_ple_gate[i], w_ple_proj[i], tiles)
        outs.append(xb)
    return jnp.stack(outs, axis=0)


def kernel(x, p, g_mix, w_in, b_f, g_q_fox, g_k_fox, w_branch_fox, w_branch_sb, w_out, g_ffn,
           w_up, conv_w, conv_b, w_down, g_ple, w_ple_gate, w_ple_proj):
    return _forward(x, p, g_mix, w_in, b_f, g_q_fox, g_k_fox, w_branch_fox, w_branch_sb, w_out,
                    g_ffn, w_up, conv_w, conv_b, w_down, g_ple, w_ple_gate, w_ple_proj,
                    _tiles(x.shape[1]))
```

```python
import functools

import jax
import jax.numpy as jnp
from jax import lax
from jax.experimental import pallas as pl
from jax.experimental.pallas import tpu as pltpu

F32 = jnp.float32
BF16 = jnp.bfloat16

EPS = 1e-6
HEAD_DIM = 128
N_HEADS = 16
W_ATT = N_HEADS * HEAD_DIM
LANES = 128
SUBLANES = 8
MXU_COLS = 256
CONV_WIDTH = 3
VMEM_LIMIT_BYTES = 56 * 1024 * 1024
UNDERFLOW = 105.0
LOG2E = 1.4426950408889634
C_SLACK = 2.0
QK_SLACK = 1.02

COL_QA, COL_KA, COL_VA = 0, 16, 32
COL_QB, COL_KB, COL_VB, COL_GA, COL_GB = 0, 16, 32, 48, 80


def _params(*sem):
    return pltpu.CompilerParams(dimension_semantics=sem, vmem_limit_bytes=VMEM_LIMIT_BYTES)


def _log_sigmoid(z):
    return jnp.minimum(z, 0.0) - jnp.log1p(jnp.exp(-jnp.abs(z)))


def _rms_norm_rows(x, g):
    ms = jnp.mean(x * x, axis=-1, keepdims=True)
    return x * lax.rsqrt(ms + EPS) * g


def _dot(a, b):
    return jnp.dot(a, b, preferred_element_type=F32)


def _dot_nt(a, b):
    return lax.dot_general(a, b, (((1,), (1,)), ((), ())), preferred_element_type=F32)


def _col_blocks(width):
    return [slice(s, s + MXU_COLS) for s in range(0, width, MXU_COLS)]


def _norm_kernel(x_ref, g_ref, o_ref):
    o_ref[...] = _rms_norm_rows(x_ref[...], g_ref[...]).astype(BF16)


def _norm_gate_kernel(x_ref, g_ref, wf_ref, bf_ref, o_ref, lf_ref):
    h = _rms_norm_rows(x_ref[...], g_ref[...]).astype(BF16)
    o_ref[...] = h
    lf_ref[...] = _log_sigmoid(_dot_nt(h, wf_ref[...]) + bf_ref[...]) * LOG2E


def _norm(x, g, *, tr, w_t=None, f_row0=None, b_f=None):
    S, D = x.shape
    row = pl.BlockSpec((tr, D), lambda i: (i, 0))
    gain = pl.BlockSpec((1, D), lambda i: (0, 0))
    if w_t is None:
        return pl.pallas_call(
            _norm_kernel, grid=(S // tr,), in_specs=[row, gain], out_specs=row,
            out_shape=jax.ShapeDtypeStruct((S, D), BF16),
            compiler_params=_params("parallel"), name="norm",
        )(x, g)
    return pl.pallas_call(
        _norm_gate_kernel, grid=(S // tr,),
        in_specs=[row, gain, pl.BlockSpec((LANES, D), lambda i: (f_row0 // LANES, 0)),
                  pl.BlockSpec((1, LANES), lambda i: (0, 0))],
        out_specs=[row, pl.BlockSpec((tr, LANES), lambda i: (i, 0))],
        out_shape=[jax.ShapeDtypeStruct((S, D), BF16), jax.ShapeDtypeStruct((S, LANES), F32)],
        compiler_params=_params("parallel"), name="norm_gate",
    )(x, g, w_t, b_f)


def _proj_sweep(h_ref, w_ref, o_ref, epilogue):
    h = h_ref[...]
    for sl in _col_blocks(w_ref.shape[0]):
        o_ref[:, sl] = epilogue(_dot_nt(h, w_ref[sl, :])).astype(BF16)


def _proj_fox_kernel(h_ref, w_ref, gq_ref, gk_ref, o_ref, *, tn):
    j = pl.program_id(1)
    tiles = W_ATT // tn

    def head_norm(g_ref, mult):
        def epilogue(acc):
            heads = [_rms_norm_rows(acc[:, c:c + HEAD_DIM], g_ref[...]) * mult
                     for c in range(0, acc.shape[1], HEAD_DIM)]
            return jnp.concatenate(heads, axis=1)
        return epilogue

    @pl.when(j < tiles)
    def _():
        _proj_sweep(h_ref, w_ref, o_ref, head_norm(gq_ref, HEAD_DIM ** -0.5 * LOG2E))

    @pl.when((j >= tiles) & (j < 2 * tiles))
    def _():
        _proj_sweep(h_ref, w_ref, o_ref, head_norm(gk_ref, 1.0))

    @pl.when(j >= 2 * tiles)
    def _():
        _proj_sweep(h_ref, w_ref, o_ref, lambda acc: acc)


def _proj_sb_kernel(h_ref, w_ref, o_ref, *, tn):
    j = pl.program_id(1)
    tiles = W_ATT // tn

    @pl.when(j < tiles)
    def _():
        _proj_sweep(h_ref, w_ref, o_ref, lambda acc: acc * (HEAD_DIM ** -0.5 * LOG2E))

    @pl.when((j >= tiles) & (j < 3 * tiles))
    def _():
        _proj_sweep(h_ref, w_ref, o_ref, lambda acc: acc)

    @pl.when(j >= 3 * tiles)
    def _():
        _proj_sweep(h_ref, w_ref, o_ref, jax.nn.sigmoid)


def _proj(kernel, name, h, w_t, row0, n, gains, *, tm, tn):
    S, D = h.shape
    return pl.pallas_call(
        functools.partial(kernel, tn=tn),
        grid=(S // tm, n // tn),
        in_specs=[pl.BlockSpec((tm, D), lambda i, j: (i, 0)),
                  pl.BlockSpec((pl.Element(tn), pl.Element(D)),
                               lambda i, j: (pl.multiple_of(row0 + j * tn, N_HEADS), 0))]
                 + [pl.BlockSpec((1, HEAD_DIM), lambda i, j: (0, 0)) for _ in gains],
        out_specs=pl.BlockSpec((tm, tn), lambda i, j: (i, j)),
        out_shape=jax.ShapeDtypeStruct((S, n), BF16),
        compiler_params=_params("parallel", "arbitrary"),
        name=name,
    )(h, w_t, *gains)


def _cumsum_kernel(x_ref, o_ref, carry_ref, *, tb):
    @pl.when(pl.program_id(0) == 0)
    def _():
        carry_ref[...] = jnp.zeros_like(carry_ref)

    x = x_ref[...]
    r = lax.broadcasted_iota(jnp.int32, (tb, tb), 0)
    c = lax.broadcasted_iota(jnp.int32, (tb, tb), 1)
    tri = (c <= r).astype(BF16)
    x1 = x.astype(BF16)
    r1 = x - x1.astype(F32)
    x2 = r1.astype(BF16)
    x3 = (r1 - x2.astype(F32)).astype(BF16)
    cs = _dot(tri, x1) + _dot(tri, x2) + _dot(tri, x3) + carry_ref[...]
    o_ref[...] = cs
    carry_ref[...] = cs[tb - 1:tb, :]


def _cumsum(x, *, tb):
    S, W = x.shape
    return pl.pallas_call(
        functools.partial(_cumsum_kernel, tb=tb),
        grid=(S // tb,),
        in_specs=[pl.BlockSpec((tb, W), lambda i: (i, 0))],
        out_specs=pl.BlockSpec((tb, W), lambda i: (i, 0)),
        out_shape=jax.ShapeDtypeStruct((S, W), F32),
        scratch_shapes=[pltpu.VMEM((1, W), F32)],
        compiler_params=_params("arbitrary"),
        name="cumsum_logf",
    )(x)


def _fox_kernel(edge_ref, lim_ref, q_ref, k_ref, v_ref, c_ref, o_ref, m_ref, l_ref, acc_ref,
                *, tq, tk, hp):
    g = pl.program_id(0)
    i = pl.program_id(1)
    nd = tq // tk
    rep = tk // LANES
    m_ref[...] = jnp.full_like(m_ref, -1e30)
    l_ref[...] = jnp.zeros_like(l_ref)
    acc_ref[...] = jnp.zeros_like(acc_ref)
    row = i * tq + lax.broadcasted_iota(jnp.int32, (tq, tk), 0)
    col = lax.broadcasted_iota(jnp.int32, (tq, tk), 1)

    def step(kb, masked):
        start = pl.multiple_of(kb * tk, tk)
        for hh in range(hp):
            sl = slice(hh * HEAD_DIM, (hh + 1) * HEAD_DIM)
            s = _dot_nt(q_ref[:, sl], k_ref[pl.ds(start, tk), sl]) - c_ref[hh, kb]
            if masked:
                s = jnp.where(start + col <= row, s, -jnp.inf)
            m_old = m_ref[hh]
            m_new = jnp.maximum(m_old, jnp.max(s, axis=-1, keepdims=True))
            alpha = jnp.exp2(m_old - m_new)
            p = jnp.exp2(s - jnp.tile(m_new, (1, rep)))
            l_ref[hh] = alpha * l_ref[hh] + jnp.sum(p, axis=-1, keepdims=True)
            acc_ref[hh] = alpha * acc_ref[hh] + _dot(p.astype(BF16), v_ref[pl.ds(start, tk), sl])
            m_ref[hh] = m_new

    for d in range(nd):
        step((i + 1) * nd - 1 - d, True)

    top = i * nd - 1
    lim = lim_ref[0]

    def chunks_needed(h):
        e_hi = edge_ref[h, jnp.maximum(top, 0)]

        def needed(kb):
            return (kb >= 0) & (e_hi - edge_ref[h, jnp.maximum(kb, 0)] > -lim)

        return top - lax.while_loop(needed, lambda kb: kb - 1, top)

    n = chunks_needed(g * hp)
    for hh in range(1, hp):
        n = jnp.maximum(n, chunks_needed(g * hp + hh))

    def body(t, carry):
        step(top - t, False)
        return carry

    lax.fori_loop(0, n, body, 0)
    for hh in range(hp):
        sl = slice(hh * HEAD_DIM, (hh + 1) * HEAD_DIM)
        o_ref[:, sl] = (acc_ref[hh] / l_ref[hh]).astype(BF16)


def _fox_attn(proj, c_rows, c_edges, lim, *, tq, tk, hp):
    S = proj.shape[0]
    nkb = S // tk
    w = hp * HEAD_DIM
    return pl.pallas_call(
        functools.partial(_fox_kernel, tq=tq, tk=tk, hp=hp),
        grid=(N_HEADS // hp, S // tq),
        in_specs=[
            pl.BlockSpec(memory_space=pltpu.SMEM),
            pl.BlockSpec(memory_space=pltpu.SMEM),
            pl.BlockSpec((tq, w), lambda g, i: (i, COL_QA // hp + g)),
            pl.BlockSpec((S, w), lambda g, i: (0, COL_KA // hp + g)),
            pl.BlockSpec((S, w), lambda g, i: (0, COL_VA // hp + g)),
            pl.BlockSpec((hp, nkb, 1, tk), lambda g, i: (g, 0, 0, 0)),
        ],
        out_specs=pl.BlockSpec((tq, w), lambda g, i: (i, g)),
        out_shape=jax.ShapeDtypeStruct((S, W_ATT), BF16),
        scratch_shapes=[
            pltpu.VMEM((hp, tq, LANES), F32),
            pltpu.VMEM((hp, tq, LANES), F32),
            pltpu.VMEM((hp, tq, HEAD_DIM), F32),
        ],
        compiler_params=_params("parallel", "arbitrary"),
        name="fox_attn",
    )(c_edges, lim, proj, proj, proj, c_rows)


def _sb_kernel(q_ref, k_ref, v_ref, o_ref, run_ref, acc_ref, *, tq, tk, hp):
    i = pl.program_id(1)
    nd = tq // tk
    rep = tk // LANES
    run_ref[...] = jnp.zeros_like(run_ref)
    acc_ref[...] = jnp.zeros_like(acc_ref)
    row = i * tq + lax.broadcasted_iota(jnp.int32, (tq, tk), 0)
    col = lax.broadcasted_iota(jnp.int32, (tq, tk), 1)
    jj = lax.broadcasted_iota(jnp.int32, (tk, tk), 0)
    ss = lax.broadcasted_iota(jnp.int32, (tk, tk), 1)
    later = (jj > ss).astype(BF16)

    def cond(carry):
        kb, run_max = carry
        return (kb >= 0) & (run_max > -UNDERFLOW * LOG2E)

    def body(carry):
        kb, _ = carry
        start = pl.multiple_of(kb * tk, tk)
        strict = start + col < row
        run_max = None
        for hh in range(hp):
            sl = slice(hh * HEAD_DIM, (hh + 1) * HEAD_DIM)
            z = _dot_nt(q_ref[:, sl], k_ref[pl.ds(start, tk), sl])
            log_beta = jnp.minimum(z, 0.0) - jnp.log2(1.0 + jnp.exp2(-jnp.abs(z)))
            l1m = jnp.where(strict, log_beta - z, 0.0)
            hi = l1m.astype(BF16)
            lo = (l1m - hi.astype(F32)).astype(BF16)
            tail = _dot(hi, later) + _dot(lo, later)
            run = run_ref[hh]
            a = jnp.where(strict, jnp.exp2(log_beta + tail + jnp.tile(run, (1, rep))), 0.0)
            acc_ref[hh] += _dot(a.astype(BF16), v_ref[pl.ds(start, tk), sl])
            run_new = run + jnp.sum(l1m, axis=-1, keepdims=True)
            run_ref[hh] = run_new
            r = jnp.max(run_new)
            run_max = r if run_max is None else jnp.maximum(run_max, r)
        return kb - 1, run_max

    lax.while_loop(cond, body, ((i + 1) * nd - 1, jnp.float32(0.0)))
    for hh in range(hp):
        sl = slice(hh * HEAD_DIM, (hh + 1) * HEAD_DIM)
        o_ref[:, sl] = acc_ref[hh].astype(BF16)


def _sb_attn(proj, *, tq, tk, hp):
    S = proj.shape[0]
    w = hp * HEAD_DIM
    return pl.pallas_call(
        functools.partial(_sb_kernel, tq=tq, tk=tk, hp=hp),
        grid=(N_HEADS // hp, S // tq),
        in_specs=[
            pl.BlockSpec((tq, w), lambda g, i: (i, COL_QB // hp + g)),
            pl.BlockSpec((S, w), lambda g, i: (0, COL_KB // hp + g)),
            pl.BlockSpec((S, w), lambda g, i: (0, COL_VB // hp + g)),
        ],
        out_specs=pl.BlockSpec((tq, w), lambda g, i: (i, g)),
        out_shape=jax.ShapeDtypeStruct((S, W_ATT), BF16),
        scratch_shapes=[
            pltpu.VMEM((hp, tq, LANES), F32),
            pltpu.VMEM((hp, tq, HEAD_DIM), F32),
        ],
        compiler_params=_params("parallel", "arbitrary"),
        name="sb_attn",
    )(proj, proj, proj)


def _merge_kernel(ya_ref, yb_ref, wa_ref, wb_ref, ga_ref, gb_ref, o_ref):
    ya = ya_ref[...]
    yb = yb_ref[...]
    for sl in _col_blocks(o_ref.shape[1]):
        a = _dot(ya, wa_ref[:, sl])
        b = _dot(yb, wb_ref[:, sl])
        o_ref[:, sl] = (ga_ref[:, sl].astype(F32) * a + gb_ref[:, sl].astype(F32) * b).astype(BF16)


def _merge(y_a, y_b, w_a, w_b, proj, *, tm, tn):
    S, K = y_a.shape
    N = w_a.shape[1]
    ga0 = COL_GA * LANES // tn
    gb0 = COL_GB * LANES // tn
    return pl.pallas_call(
        _merge_kernel,
        grid=(S // tm, N // tn),
        in_specs=[
            pl.BlockSpec((tm, K), lambda i, j: (i, 0)),
            pl.BlockSpec((tm, K), lambda i, j: (i, 0)),
            pl.BlockSpec((K, tn), lambda i, j: (0, j)),
            pl.BlockSpec((K, tn), lambda i, j: (0, j)),
            pl.BlockSpec((tm, tn), lambda i, j: (i, ga0 + j)),
            pl.BlockSpec((tm, tn), lambda i, j: (i, gb0 + j)),
        ],
        out_specs=pl.BlockSpec((tm, tn), lambda i, j: (i, j)),
        out_shape=jax.ShapeDtypeStruct((S, N), BF16),
        compiler_params=_params("parallel", "arbitrary"),
        name="merge",
    )(y_a, y_b, w_a, w_b, proj, proj)


def _matmul_res_kernel(a_ref, w_ref, x_ref, o_ref):
    a = a_ref[...]
    for sl in _col_blocks(o_ref.shape[1]):
        o_ref[:, sl] = x_ref[:, sl] + _dot(a, w_ref[:, sl])


def _matmul_res(a, w, x, *, tm, tn):
    S, K = a.shape
    N = w.shape[1]
    return pl.pallas_call(
        _matmul_res_kernel,
        grid=(S // tm, N // tn),
        in_specs=[
            pl.BlockSpec((tm, K), lambda i, j: (i, 0)),
            pl.BlockSpec((K, tn), lambda i, j: (0, j)),
            pl.BlockSpec((tm, tn), lambda i, j: (i, j)),
        ],
        out_specs=pl.BlockSpec((tm, tn), lambda i, j: (i, j)),
        out_shape=jax.ShapeDtypeStruct((S, N), F32),
        compiler_params=_params("parallel", "arbitrary"),
        name="out_res",
    )(a, w, x)


def _up_glu_kernel(h_ref, wg_ref, wv_ref, cwg_ref, cwv_ref, cbg_ref, cbv_ref,
                   o_ref, ug_ref, uv_ref, carry_g_ref, carry_v_ref, *, tm, ts):
    i = pl.program_id(0)
    j = pl.program_id(1)
    halo = SUBLANES
    first = i == 0

    def conv(r0, w_ref, cw_ref, cb_ref, u_ref, carry_ref):
        u = _dot(h_ref[r0:r0 + ts, :], w_ref[...])
        if r0 == 0:
            u_ref[0:halo, :] = jnp.where(first, 0.0, carry_ref[j])
        u_ref[halo + r0:halo + r0 + ts, :] = u
        if r0 + ts == tm:
            carry_ref[j] = u[ts - halo:ts, :]
        cw = cw_ref[...]
        return (cb_ref[...]
                + cw[0:1, :] * u_ref[halo - 2 + r0:halo - 2 + r0 + ts, :]
                + cw[1:2, :] * u_ref[halo - 1 + r0:halo - 1 + r0 + ts, :]
                + cw[2:3, :] * u)

    for r0 in range(0, tm, ts):
        gate = conv(r0, wg_ref, cwg_ref, cbg_ref, ug_ref, carry_g_ref)
        val = conv(r0, wv_ref, cwv_ref, cbv_ref, uv_ref, carry_v_ref)
        o_ref[r0:r0 + ts, :] = (gate * jax.nn.sigmoid(gate) * val).astype(BF16)


def _up_glu(h, w_up, conv_w, conv_b, *, tm, tn, ts):
    S, D = h.shape
    d_ff = w_up.shape[1] // 2
    nj = d_ff // tn
    g_map = lambda i, j: (0, j)
    v_map = lambda i, j: (0, nj + j)
    return pl.pallas_call(
        functools.partial(_up_glu_kernel, tm=tm, ts=ts),
        grid=(S // tm, nj),
        in_specs=[
            pl.BlockSpec((tm, D), lambda i, j: (i, 0)),
            pl.BlockSpec((D, tn), g_map), pl.BlockSpec((D, tn), v_map),
            pl.BlockSpec((CONV_WIDTH, tn), g_map), pl.BlockSpec((CONV_WIDTH, tn), v_map),
            pl.BlockSpec((1, tn), g_map), pl.BlockSpec((1, tn), v_map),
        ],
        out_specs=pl.BlockSpec((tm, tn), lambda i, j: (i, j)),
        out_shape=jax.ShapeDtypeStruct((S, d_ff), BF16),
        scratch_shapes=[
            pltpu.VMEM((tm + SUBLANES, tn), F32),
            pltpu.VMEM((tm + SUBLANES, tn), F32),
            pltpu.VMEM((nj, SUBLANES, tn), F32),
            pltpu.VMEM((nj, SUBLANES, tn), F32),
        ],
        compiler_params=_params("arbitrary", "arbitrary"),
        name="up_glu",
    )(h, w_up, w_up, conv_w, conv_w, conv_b, conv_b)


def _down_res_kernel(a_ref, w_ref, x_ref, o_ref, acc_ref):
    k = pl.program_id(2)

    @pl.when(k == 0)
    def _():
        acc_ref[...] = x_ref[...]

    acc_ref[...] += _dot(a_ref[...], w_ref[...])

    @pl.when(k == pl.num_programs(2) - 1)
    def _():
        o_ref[...] = acc_ref[...]


def _down_res(a, w, x, *, tm, tn, tk):
    S, K = a.shape
    N = w.shape[1]
    return pl.pallas_call(
        _down_res_kernel,
        grid=(S // tm, N // tn, K // tk),
        in_specs=[
            pl.BlockSpec((tm, tk), lambda i, j, k: (i, k)),
            pl.BlockSpec((tk, tn), lambda i, j, k: (k, j)),
            pl.BlockSpec((tm, tn), lambda i, j, k: (i, j)),
        ],
        out_specs=pl.BlockSpec((tm, tn), lambda i, j, k: (i, j)),
        out_shape=jax.ShapeDtypeStruct((S, N), F32),
        scratch_shapes=[pltpu.VMEM((tm, tn), F32)],
        compiler_params=_params("parallel", "parallel", "arbitrary"),
        name="down_res",
    )(a, w, x)


def _ple_kernel(h_ref, wg_ref, p_ref, wp_ref, x_ref, o_ref):
    h = h_ref[...]
    p = p_ref[...].astype(BF16)
    for sl in _col_blocks(o_ref.shape[1]):
        gate = jax.nn.sigmoid(_dot(h, wg_ref[:, sl]))
        o_ref[:, sl] = x_ref[:, sl] + gate * _dot(p, wp_ref[:, sl])


def _ple(h, w_gate, p, w_proj, x, *, tm, tn):
    S, D = h.shape
    P = p.shape[1]
    return pl.pallas_call(
        _ple_kernel,
        grid=(S // tm, D // tn),
        in_specs=[
            pl.BlockSpec((tm, D), lambda i, j: (i, 0)),
            pl.BlockSpec((D, tn), lambda i, j: (0, j)),
            pl.BlockSpec((tm, P), lambda i, j: (i, 0)),
            pl.BlockSpec((P, tn), lambda i, j: (0, j)),
            pl.BlockSpec((tm, tn), lambda i, j: (i, j)),
        ],
        out_specs=pl.BlockSpec((tm, tn), lambda i, j: (i, j)),
        out_shape=jax.ShapeDtypeStruct((S, D), F32),
        compiler_params=_params("parallel", "arbitrary"),
        name="ple",
    )(h, w_gate, p, w_proj, x)


def _tiles(S):
    return dict(
        norm=dict(tr=min(256, S)),
        proj=dict(tm=min(1024, S), tn=1024),
        cumsum=dict(tb=min(512, S)),
        fox=dict(tq=min(512, S), tk=min(512, S), hp=2),
        sb=dict(tq=min(256, S), tk=min(256, S), hp=2),
        merge=dict(tm=min(1024, S), tn=512),
        out=dict(tm=min(1024, S), tn=512),
        up=dict(tm=min(2048, S), tn=256, ts=min(1024, S)),
        down=dict(tm=min(1024, S), tn=512, tk=5504),
        ple=dict(tm=min(1024, S), tn=512),
    )


def _layer(x, p, g_mix, w_in, b_f, g_q, g_k, w_bf, w_bs, w_out, g_ffn, w_up, conv_w, conv_b,
           w_down, g_ple, w_ple_gate, w_ple_proj, tiles):
    S, D = x.shape
    f0 = 3 * W_ATT
    sb0 = f0 + N_HEADS
    w_t = w_in.T.astype(BF16)
    b_f_row = jnp.pad(b_f[None, :], ((0, 0), (0, LANES - N_HEADS)))

    h, log_f = _norm(x, g_mix[None, :], w_t=w_t, f_row0=f0, b_f=b_f_row, **tiles["norm"])
    proj_a = _proj(_proj_fox_kernel, "proj_fox", h, w_t, 0, f0, (g_q[None, :], g_k[None, :]),
                   **tiles["proj"])
    proj_b = _proj(_proj_sb_kernel, "proj_sb", h, w_t, sb0, w_t.shape[0] - sb0, (),
                   **tiles["proj"])
    c = _cumsum(log_f, **tiles["cumsum"])
    tk = tiles["fox"]["tk"]
    c_rows = c[:, :N_HEADS].T.reshape(N_HEADS, S // tk, 1, tk)
    c_edges = c_rows[:, :, 0, tk - 1]
    qk_max = HEAD_DIM ** 0.5 * LOG2E * QK_SLACK * jnp.max(jnp.abs(g_q)) * jnp.max(jnp.abs(g_k))
    lim = ((UNDERFLOW + C_SLACK) * LOG2E + 2.0 * qk_max).reshape(1)
    y_a = _fox_attn(proj_a, c_rows, c_edges, lim, **tiles["fox"])
    y_b = _sb_attn(proj_b, **tiles["sb"])
    merged = _merge(y_a, y_b, w_bf.astype(BF16), w_bs.astype(BF16), proj_b, **tiles["merge"])
    x = _matmul_res(merged, w_out.astype(BF16), x, **tiles["out"])
    h = _norm(x, g_ffn[None, :], **tiles["norm"])
    gated = _up_glu(h, w_up.astype(BF16), conv_w, conv_b[None, :], **tiles["up"])
    x = _down_res(gated, w_down.astype(BF16), x, **tiles["down"])
    h = _norm(x, g_ple[None, :], **tiles["norm"])
    x = _ple(h, w_ple_gate.astype(BF16), p, w_ple_proj.astype(BF16), x, **tiles["ple"])
    return x


def _forward(x, p, g_mix, w_in, b_f, g_q_fox, g_k_fox, w_branch_fox, w_branch_sb, w_out, g_ffn,
             w_up, conv_w, conv_b, w_down, g_ple, w_ple_gate, w_ple_proj, tiles):
    B, S, D = x.shape
    outs = []
    for b in range(B):
        xb = x[b]
        for i in range(w_in.shape[0]):
            xb = _layer(xb, p[i, b], g_mix[i], w_in[i], b_f[i], g_q_fox[i], g_k_fox[i],
                        w_branch_fox[i], w_branch_sb[i], w_out[i], g_ffn[i], w_up[i], conv_w[i],
                        conv_b[i], w_down[i], g_ple[i], w_ple_gate[i], w_ple_proj[i], tiles)
        outs.append(xb)
    return jnp.stack(outs, axis=0)


def kernel(x, p, g_mix, w_in, b_f, g_q_fox, g_k_fox, w_branch_fox, w_branch_sb, w_out, g_ffn,
           w_up, conv_w, conv_b, w_down, g_ple, w_ple_gate, w_ple_proj):
    return _forward(x, p, g_mix, w_in, b_f, g_q_fox, g_k_fox, w_branch_fox, w_branch_sb, w_out,
                    g_ffn, w_up, conv_w, conv_b, w_down, g_ple, w_ple_gate, w_ple_proj,
                    _tiles(x.shape[1]))
```

```python
import functools

import jax
import jax.numpy as jnp
from jax import lax
from jax.experimental import pallas as pl
from jax.experimental.pallas import tpu as pltpu

F32 = jnp.float32
BF16 = jnp.bfloat16

EPS = 1e-6
HEAD_DIM = 128
N_HEADS = 16
W_ATT = N_HEADS * HEAD_DIM
LANES = 128
SUBLANES = 8
MXU_COLS = 256
CONV_WIDTH = 3
VMEM_LIMIT_BYTES = 56 * 1024 * 1024
UNDERFLOW = 105.0
LOG2E = 1.4426950408889634
C_SLACK = 2.0
QK_SLACK = 1.02

COL_QA, COL_KA, COL_VA = 0, 16, 32
COL_QB, COL_KB, COL_VB, COL_GA, COL_GB = 0, 16, 32, 48, 80


def _params(*sem):
    return pltpu.CompilerParams(dimension_semantics=sem, vmem_limit_bytes=VMEM_LIMIT_BYTES)


def _log_sigmoid(z):
    return jnp.minimum(z, 0.0) - jnp.log1p(jnp.exp(-jnp.abs(z)))


def _rms_norm_rows(x, g):
    ms = jnp.mean(x * x, axis=-1, keepdims=True)
    return x * lax.rsqrt(ms + EPS) * g


def _dot(a, b):
    return jnp.dot(a, b, preferred_element_type=F32)


def _dot_nt(a, b):
    return lax.dot_general(a, b, (((1,), (1,)), ((), ())), preferred_element_type=F32)


def _col_blocks(width):
    return [slice(s, s + MXU_COLS) for s in range(0, width, MXU_COLS)]


def _norm_kernel(x_ref, g_ref, o_ref):
    o_ref[...] = _rms_norm_rows(x_ref[...], g_ref[...]).astype(BF16)


def _norm_gate_kernel(x_ref, g_ref, wf_ref, bf_ref, o_ref, lf_ref):
    h = _rms_norm_rows(x_ref[...], g_ref[...]).astype(BF16)
    o_ref[...] = h
    lf_ref[...] = _log_sigmoid(_dot_nt(h, wf_ref[...]) + bf_ref[...]) * LOG2E


def _norm(x, g, *, tr, w_t=None, f_row0=None, b_f=None):
    S, D = x.shape
    row = pl.BlockSpec((tr, D), lambda i: (i, 0))
    gain = pl.BlockSpec((1, D), lambda i: (0, 0))
    if w_t is None:
        return pl.pallas_call(
            _norm_kernel, grid=(S // tr,), in_specs=[row, gain], out_specs=row,
            out_shape=jax.ShapeDtypeStruct((S, D), BF16),
            compiler_params=_params("parallel"), name="norm",
        )(x, g)
    return pl.pallas_call(
        _norm_gate_kernel, grid=(S // tr,),
        in_specs=[row, gain, pl.BlockSpec((LANES, D), lambda i: (f_row0 // LANES, 0)),
                  pl.BlockSpec((1, LANES), lambda i: (0, 0))],
        out_specs=[row, pl.BlockSpec((tr, LANES), lambda i: (i, 0))],
        out_shape=[jax.ShapeDtypeStruct((S, D), BF16), jax.ShapeDtypeStruct((S, LANES), F32)],
        compiler_params=_params("parallel"), name="norm_gate",
    )(x, g, w_t, b_f)


def _proj_sweep(h_ref, w_ref, o_ref, epilogue):
    h = h_ref[...]
    for sl in _col_blocks(w_ref.shape[0]):
        o_ref[:, sl] = epilogue(_dot_nt(h, w_ref[sl, :])).astype(BF16)


def _proj_fox_kernel(h_ref, w_ref, gq_ref, gk_ref, o_ref, *, tn):
    j = pl.program_id(1)
    tiles = W_ATT // tn

    def head_norm(g_ref, mult):
        def epilogue(acc):
            heads = [_rms_norm_rows(acc[:, c:c + HEAD_DIM], g_ref[...]) * mult
                     for c in range(0, acc.shape[1], HEAD_DIM)]
            return jnp.concatenate(heads, axis=1)
        return epilogue

    @pl.when(j < tiles)
    def _():
        _proj_sweep(h_ref, w_ref, o_ref, head_norm(gq_ref, HEAD_DIM ** -0.5 * LOG2E))

    @pl.when((j >= tiles) & (j < 2 * tiles))
    def _():
        _proj_sweep(h_ref, w_ref, o_ref, head_norm(gk_ref, 1.0))

    @pl.when(j >= 2 * tiles)
    def _():
        _proj_sweep(h_ref, w_ref, o_ref, lambda acc: acc)


def _proj_sb_kernel(h_ref, w_ref, o_ref, *, tn):
    j = pl.program_id(1)
    tiles = W_ATT // tn

    @pl.when(j < tiles)
    def _():
        _proj_sweep(h_ref, w_ref, o_ref, lambda acc: acc * (HEAD_DIM ** -0.5 * LOG2E))

    @pl.when((j >= tiles) & (j < 3 * tiles))
    def _():
        _proj_sweep(h_ref, w_ref, o_ref, lambda acc: acc)

    @pl.when(j >= 3 * tiles)
    def _():
        _proj_sweep(h_ref, w_ref, o_ref, jax.nn.sigmoid)


def _proj(kernel, name, h, w_t, row0, n, gains, *, tm, tn):
    S, D = h.shape
    return pl.pallas_call(
        functools.partial(kernel, tn=tn),
        grid=(S // tm, n // tn),
        in_specs=[pl.BlockSpec((tm, D), lambda i, j: (i, 0)),
                  pl.BlockSpec((pl.Element(tn), pl.Element(D)),
                               lambda i, j: (pl.multiple_of(row0 + j * tn, N_HEADS), 0))]
                 + [pl.BlockSpec((1, HEAD_DIM), lambda i, j: (0, 0)) for _ in gains],
        out_specs=pl.BlockSpec((tm, tn), lambda i, j: (i, j)),
        out_shape=jax.ShapeDtypeStruct((S, n), BF16),
        compiler_params=_params("parallel", "arbitrary"),
        name=name,
    )(h, w_t, *gains)


def _cumsum_kernel(x_ref, o_ref, carry_ref, *, tb):
    @pl.when(pl.program_id(0) == 0)
    def _():
        carry_ref[...] = jnp.zeros_like(carry_ref)

    x = x_ref[...]
    r = lax.broadcasted_iota(jnp.int32, (tb, tb), 0)
    c = lax.broadcasted_iota(jnp.int32, (tb, tb), 1)
    tri = (c <= r).astype(BF16)
    x1 = x.astype(BF16)
    r1 = x - x1.astype(F32)
    x2 = r1.astype(BF16)
    x3 = (r1 - x2.astype(F32)).astype(BF16)
    cs = _dot(tri, x1) + _dot(tri, x2) + _dot(tri, x3) + carry_ref[...]
    o_ref[...] = cs
    carry_ref[...] = cs[tb - 1:tb, :]


def _cumsum(x, *, tb):
    S, W = x.shape
    return pl.pallas_call(
        functools.partial(_cumsum_kernel, tb=tb),
        grid=(S // tb,),
        in_specs=[pl.BlockSpec((tb, W), lambda i: (i, 0))],
        out_specs=pl.BlockSpec((tb, W), lambda i: (i, 0)),
        out_shape=jax.ShapeDtypeStruct((S, W), F32),
        scratch_shapes=[pltpu.VMEM((1, W), F32)],
        compiler_params=_params("arbitrary"),
        name="cumsum_logf",
    )(x)


def _fox_kernel(edge_ref, lim_ref, q_ref, k_ref, v_ref, c_ref, o_ref, m_ref, l_ref, acc_ref,
                *, tq, tk, hp):
    g = pl.program_id(0)
    i = pl.program_id(1)
    nd = tq // tk
    rep = tk // LANES
    m_ref[...] = jnp.full_like(m_ref, -1e30)
    l_ref[...] = jnp.zeros_like(l_ref)
    acc_ref[...] = jnp.zeros_like(acc_ref)
    row = i * tq + lax.broadcasted_iota(jnp.int32, (tq, tk), 0)
    col = lax.broadcasted_iota(jnp.int32, (tq, tk), 1)

    def step(kb, masked):
        start = pl.multiple_of(kb * tk, tk)
        for hh in range(hp):
            sl = slice(hh * HEAD_DIM, (hh + 1) * HEAD_DIM)
            s = _dot_nt(q_ref[:, sl], k_ref[pl.ds(start, tk), sl]) - c_ref[hh, kb]
            if masked:
                s = jnp.where(start + col <= row, s, -jnp.inf)
            m_old = m_ref[hh]
            m_new = jnp.maximum(m_old, jnp.max(s, axis=-1, keepdims=True))
            alpha = jnp.exp2(m_old - m_new)
            p = jnp.exp2(s - jnp.tile(m_new, (1, rep)))
            l_ref[hh] = alpha * l_ref[hh] + jnp.sum(p, axis=-1, keepdims=True)
            acc_ref[hh] = alpha * acc_ref[hh] + _dot(p.astype(BF16), v_ref[pl.ds(start, tk), sl])
            m_ref[hh] = m_new

    for d in range(nd):
        step((i + 1) * nd - 1 - d, True)

    top = i * nd - 1
    lim = lim_ref[0]

    def chunks_needed(h):
        e_hi = edge_ref[h, jnp.maximum(top, 0)]

        def needed(kb):
            return (kb >= 0) & (e_hi - edge_ref[h, jnp.maximum(kb, 0)] > -lim)

        return top - lax.while_loop(needed, lambda kb: kb - 1, top)

    n = chunks_needed(g * hp)
    for hh in range(1, hp):
        n = jnp.maximum(n, chunks_needed(g * hp + hh))

    def body(t, carry):
        step(top - t, False)
        return carry

    lax.fori_loop(0, n, body, 0)
    for hh in range(hp):
        sl = slice(hh * HEAD_DIM, (hh + 1) * HEAD_DIM)
        o_ref[:, sl] = (acc_ref[hh] / l_ref[hh]).astype(BF16)


def _fox_attn(proj, c_rows, c_edges, lim, *, tq, tk, hp):
    S = proj.shape[0]
    nkb = S // tk
    w = hp * HEAD_DIM
    return pl.pallas_call(
        functools.partial(_fox_kernel, tq=tq, tk=tk, hp=hp),
        grid=(N_HEADS // hp, S // tq),
        in_specs=[
            pl.BlockSpec(memory_space=pltpu.SMEM),
            pl.BlockSpec(memory_space=pltpu.SMEM),
            pl.BlockSpec((tq, w), lambda g, i: (i, COL_QA // hp + g)),
            pl.BlockSpec((S, w), lambda g, i: (0, COL_KA // hp + g)),
            pl.BlockSpec((S, w), lambda g, i: (0, COL_VA // hp + g)),
            pl.BlockSpec((hp, nkb, 1, tk), lambda g, i: (g, 0, 0, 0)),
        ],
        out_specs=pl.BlockSpec((tq, w), lambda g, i: (i, g)),
        out_shape=jax.ShapeDtypeStruct((S, W_ATT), BF16),
        scratch_shapes=[
            pltpu.VMEM((hp, tq, LANES), F32),
            pltpu.VMEM((hp, tq, LANES), F32),
            pltpu.VMEM((hp, tq, HEAD_DIM), F32),
        ],
        compiler_params=_params("parallel", "arbitrary"),
        name="fox_attn",
    )(c_edges, lim, proj, proj, proj, c_rows)


def _sb_kernel(q_ref, k_ref, v_ref, o_ref, run_ref, acc_ref, *, tq, tk, hp):
    i = pl.program_id(1)
    nd = tq // tk
    rep = tk // LANES
    run_ref[...] = jnp.zeros_like(run_ref)
    acc_ref[...] = jnp.zeros_like(acc_ref)
    row = i * tq + lax.broadcasted_iota(jnp.int32, (tq, tk), 0)
    col = lax.broadcasted_iota(jnp.int32, (tq, tk), 1)
    jj = lax.broadcasted_iota(jnp.int32, (tk, tk), 0)
    ss = lax.broadcasted_iota(jnp.int32, (tk, tk), 1)
    later = (jj > ss).astype(BF16)

    def cond(carry):
        kb, run_max = carry
        return (kb >= 0) & (run_max > -UNDERFLOW * LOG2E)

    def body(carry):
        kb, _ = carry
        start = pl.multiple_of(kb * tk, tk)
        strict = start + col < row
        run_max = None
        for hh in range(hp):
            sl = slice(hh * HEAD_DIM, (hh + 1) * HEAD_DIM)
            z = _dot_nt(q_ref[:, sl], k_ref[pl.ds(start, tk), sl])
            log_beta = jnp.minimum(z, 0.0) - jnp.log2(1.0 + jnp.exp2(-jnp.abs(z)))
            l1m = jnp.where(strict, log_beta - z, 0.0)
            hi = l1m.astype(BF16)
            lo = (l1m - hi.astype(F32)).astype(BF16)
            tail = _dot(hi, later) + _dot(lo, later)
            run = run_ref[hh]
            a = jnp.where(strict, jnp.exp2(log_beta + tail + jnp.tile(run, (1, rep))), 0.0)
            acc_ref[hh] += _dot(a.astype(BF16), v_ref[pl.ds(start, tk), sl])
            run_new = run + jnp.sum(l1m, axis=-1, keepdims=True)
            run_ref[hh] = run_new
            r = jnp.max(run_new)
            run_max = r if run_max is None else jnp.maximum(run_max, r)
        return kb - 1, run_max

    lax.while_loop(cond, body, ((i + 1) * nd - 1, jnp.float32(0.0)))
    for hh in range(hp):
        sl = slice(hh * HEAD_DIM, (hh + 1) * HEAD_DIM)
        o_ref[:, sl] = acc_ref[hh].astype(BF16)


def _sb_attn(proj, *, tq, tk, hp):
    S = proj.shape[0]
    w = hp * HEAD_DIM
    return pl.pallas_call(
        functools.partial(_sb_kernel, tq=tq, tk=tk, hp=hp),
        grid=(N_HEADS // hp, S // tq),
        in_specs=[
            pl.BlockSpec((tq, w), lambda g, i: (i, COL_QB // hp + g)),
            pl.BlockSpec((S, w), lambda g, i: (0, COL_KB // hp + g)),
            pl.BlockSpec((S, w), lambda g, i: (0, COL_VB // hp + g)),
        ],
        out_specs=pl.BlockSpec((tq, w), lambda g, i: (i, g)),
        out_shape=jax.ShapeDtypeStruct((S, W_ATT), BF16),
        scratch_shapes=[
            pltpu.VMEM((hp, tq, LANES), F32),
            pltpu.VMEM((hp, tq, HEAD_DIM), F32),
        ],
        compiler_params=_params("parallel", "arbitrary"),
        name="sb_attn",
    )(proj, proj, proj)


def _merge_kernel(ya_ref, yb_ref, wa_ref, wb_ref, ga_ref, gb_ref, o_ref):
    ya = ya_ref[...]
    yb = yb_ref[...]
    for sl in _col_blocks(o_ref.shape[1]):
        a = _dot(ya, wa_ref[:, sl])
        b = _dot(yb, wb_ref[:, sl])
        o_ref[:, sl] = (ga_ref[:, sl].astype(F32) * a + gb_ref[:, sl].astype(F32) * b).astype(BF16)


def _merge(y_a, y_b, w_a, w_b, proj, *, tm, tn):
    S, K = y_a.shape
    N = w_a.shape[1]
    ga0 = COL_GA * LANES // tn
    gb0 = COL_GB * LANES // tn
    return pl.pallas_call(
        _merge_kernel,
        grid=(S // tm, N // tn),
        in_specs=[
            pl.BlockSpec((tm, K), lambda i, j: (i, 0)),
            pl.BlockSpec((tm, K), lambda i, j: (i, 0)),
            pl.BlockSpec((K, tn), lambda i, j: (0, j)),
            pl.BlockSpec((K, tn), lambda i, j: (0, j)),
            pl.BlockSpec((tm, tn), lambda i, j: (i, ga0 + j)),
            pl.BlockSpec((tm, tn), lambda i, j: (i, gb0 + j)),
        ],
        out_specs=pl.BlockSpec((tm, tn), lambda i, j: (i, j)),
        out_shape=jax.ShapeDtypeStruct((S, N), BF16),
        compiler_params=_params("parallel", "arbitrary"),
        name="merge",
    )(y_a, y_b, w_a, w_b, proj, proj)


def _matmul_res_kernel(a_ref, w_ref, x_ref, o_ref):
    a = a_ref[...]
    for sl in _col_blocks(o_ref.shape[1]):
        o_ref[:, sl] = x_ref[:, sl] + _dot(a, w_ref[:, sl])


def _matmul_res(a, w, x, *, tm, tn):
    S, K = a.shape
    N = w.shape[1]
    return pl.pallas_call(
        _matmul_res_kernel,
        grid=(S // tm, N // tn),
        in_specs=[
            pl.BlockSpec((tm, K), lambda i, j: (i, 0)),
            pl.BlockSpec((K, tn), lambda i, j: (0, j)),
            pl.BlockSpec((tm, tn), lambda i, j: (i, j)),
        ],
        out_specs=pl.BlockSpec((tm, tn), lambda i, j: (i, j)),
        out_shape=jax.ShapeDtypeStruct((S, N), F32),
        compiler_params=_params("parallel", "arbitrary"),
        name="out_res",
    )(a, w, x)


def _up_glu_kernel(h_ref, wg32_ref, wv32_ref, cwg_ref, cwv_ref, cbg_ref, cbv_ref, o_ref,
                   wg0_ref, wg1_ref, wv0_ref, wv1_ref, ug0_ref, ug1_ref, uv0_ref, uv1_ref,
                   carry_g_ref, carry_v_ref, *, tm, ts, nj, n_tiles):
    t = pl.program_id(0)
    tile = jnp.clip(t - 1, 0, n_tiles - 1)
    j = tile % nj
    wg_refs, wv_refs = (wg0_ref, wg1_ref), (wv0_ref, wv1_ref)
    ug_refs, uv_refs = (ug0_ref, ug1_ref), (uv0_ref, uv1_ref)
    tc = min(ts, 4 * SUBLANES)
    kc = h_ref.shape[1] * tc // ts
    halo = SUBLANES
    first = tile // nj == 0

    @pl.when(t == 0)
    def _():
        wg1_ref[...] = jnp.zeros_like(wg1_ref)
        wv1_ref[...] = jnp.zeros_like(wv1_ref)
        ug0_ref[...] = jnp.zeros_like(ug0_ref)
        uv0_ref[...] = jnp.zeros_like(uv0_ref)

    def cast_chunk(slot, ks):
        wg_refs[slot][ks, :] = wg32_ref[ks, :].astype(BF16)
        wv_refs[slot][ks, :] = wv32_ref[ks, :].astype(BF16)

    def epilogue_chunk(slot, r):
        def conv(u_ref, cw_ref, cb_ref):
            cw = cw_ref[...]
            rows = lambda back: u_ref[halo - back + r:halo - back + r + tc, :]
            return cb_ref[...] + cw[0:1, :] * rows(2) + cw[1:2, :] * rows(1) + cw[2:3, :] * rows(0)

        gate = conv(ug_refs[slot], cwg_ref, cbg_ref)
        val = conv(uv_refs[slot], cwv_ref, cbv_ref)
        o_ref[r:r + tc, :] = (gate * jax.nn.sigmoid(gate) * val).astype(BF16)

    def rows_step(p, r0):
        acc_g = acc_v = None
        for c in range(ts // tc):
            ks = slice(c * kc, (c + 1) * kc)
            lhs = h_ref[r0:r0 + ts, ks]
            dg = _dot(lhs, wg_refs[1 - p][ks, :])
            dv = _dot(lhs, wv_refs[1 - p][ks, :])
            acc_g = dg if acc_g is None else acc_g + dg
            acc_v = dv if acc_v is None else acc_v + dv
            if r0 == 0:
                cast_chunk(p, ks)
            epilogue_chunk(p, r0 + c * tc)
        for u, u_ref, carry_ref in ((acc_g, ug_refs[1 - p], carry_g_ref),
                                    (acc_v, uv_refs[1 - p], carry_v_ref)):
            if r0 == 0:
                u_ref[0:halo, :] = jnp.where(first, 0.0, carry_ref[j])
            u_ref[halo + r0:halo + r0 + ts, :] = u
            if r0 + ts == tm:
                carry_ref[j] = u[ts - halo:ts, :]

    for parity in (0, 1):
        @pl.when(t % 2 == parity)
        def _():
            for r0 in range(0, tm, ts):
                rows_step(parity, r0)


def _up_glu(h, w_up, conv_w, conv_b, *, tm, tn, ts):
    S, D = h.shape
    d_ff = w_up.shape[1] // 2
    nj = d_ff // tn
    n_tiles = (S // tm) * nj
    cast = lambda t: jnp.minimum(t, n_tiles - 1)
    mult = lambda t: jnp.clip(t - 1, 0, n_tiles - 1)
    done = lambda t: jnp.clip(t - 2, 0, n_tiles - 1)
    wbuf = pltpu.VMEM((D, tn), BF16)
    ubuf = pltpu.VMEM((tm + SUBLANES, tn), F32)
    return pl.pallas_call(
        functools.partial(_up_glu_kernel, tm=tm, ts=ts, nj=nj, n_tiles=n_tiles),
        grid=(n_tiles + 2,),
        in_specs=[
            pl.BlockSpec((tm, D), lambda t: (mult(t) // nj, 0), pipeline_mode=pl.Buffered(1)),
            pl.BlockSpec((D, tn), lambda t: (0, cast(t) % nj)),
            pl.BlockSpec((D, tn), lambda t: (0, nj + cast(t) % nj)),
            pl.BlockSpec((CONV_WIDTH, tn), lambda t: (0, done(t) % nj)),
            pl.BlockSpec((CONV_WIDTH, tn), lambda t: (0, nj + done(t) % nj)),
            pl.BlockSpec((1, tn), lambda t: (0, done(t) % nj)),
            pl.BlockSpec((1, tn), lambda t: (0, nj + done(t) % nj)),
        ],
        out_specs=pl.BlockSpec((tm, tn), lambda t: (done(t) // nj, done(t) % nj)),
        out_shape=jax.ShapeDtypeStruct((S, d_ff), BF16),
        scratch_shapes=[wbuf, wbuf, wbuf, wbuf, ubuf, ubuf, ubuf, ubuf,
                        pltpu.VMEM((nj, SUBLANES, tn), F32),
                        pltpu.VMEM((nj, SUBLANES, tn), F32)],
        compiler_params=_params("arbitrary"),
        name="up_glu",
    )(h, w_up, w_up, conv_w, conv_w, conv_b, conv_b)


def _down_res_kernel(a_ref, w_ref, x_ref, o_ref, acc_ref):
    k = pl.program_id(2)

    @pl.when(k == 0)
    def _():
        acc_ref[...] = x_ref[...]

    acc_ref[...] += _dot(a_ref[...], w_ref[...])

    @pl.when(k == pl.num_programs(2) - 1)
    def _():
        o_ref[...] = acc_ref[...]


def _down_res(a, w, x, *, tm, tn, tk):
    S, K = a.shape
    N = w.shape[1]
    return pl.pallas_call(
        _down_res_kernel,
        grid=(S // tm, N // tn, K // tk),
        in_specs=[
            pl.BlockSpec((tm, tk), lambda i, j, k: (i, k)),
            pl.BlockSpec((tk, tn), lambda i, j, k: (k, j)),
            pl.BlockSpec((tm, tn), lambda i, j, k: (i, j)),
        ],
        out_specs=pl.BlockSpec((tm, tn), lambda i, j, k: (i, j)),
        out_shape=jax.ShapeDtypeStruct((S, N), F32),
        scratch_shapes=[pltpu.VMEM((tm, tn), F32)],
        compiler_params=_params("parallel", "parallel", "arbitrary"),
        name="down_res",
    )(a, w, x)


def _ple_kernel(h_ref, wg_ref, p_ref, wp_ref, x_ref, o_ref):
    h = h_ref[...]
    p = p_ref[...].astype(BF16)
    for sl in _col_blocks(o_ref.shape[1]):
        gate = jax.nn.sigmoid(_dot(h, wg_ref[:, sl]))
        o_ref[:, sl] = x_ref[:, sl] + gate * _dot(p, wp_ref[:, sl])


def _ple(h, w_gate, p, w_proj, x, *, tm, tn):
    S, D = h.shape
    P = p.shape[1]
    return pl.pallas_call(
        _ple_kernel,
        grid=(S // tm, D // tn),
        in_specs=[
            pl.BlockSpec((tm, D), lambda i, j: (i, 0)),
            pl.BlockSpec((D, tn), lambda i, j: (0, j)),
            pl.BlockSpec((tm, P), lambda i, j: (i, 0)),
            pl.BlockSpec((P, tn), lambda i, j: (0, j)),
            pl.BlockSpec((tm, tn), lambda i, j: (i, j)),
        ],
        out_specs=pl.BlockSpec((tm, tn), lambda i, j: (i, j)),
        out_shape=jax.ShapeDtypeStruct((S, D), F32),
        compiler_params=_params("parallel", "arbitrary"),
        name="ple",
    )(h, w_gate, p, w_proj, x)


def _tiles(S):
    return dict(
        norm=dict(tr=min(256, S)),
        proj=dict(tm=min(1024, S), tn=1024),
        cumsum=dict(tb=min(512, S)),
        fox=dict(tq=min(512, S), tk=min(512, S), hp=2),
        sb=dict(tq=min(256, S), tk=min(256, S), hp=4),
        merge=dict(tm=min(1024, S), tn=512),
        out=dict(tm=min(1024, S), tn=512),
        up=dict(tm=min(2048, S), tn=256, ts=min(512, S)),
        down=dict(tm=min(1024, S), tn=512, tk=5504),
        ple=dict(tm=min(1024, S), tn=512),
    )


def _layer(x, p, g_mix, w_in, b_f, g_q, g_k, w_bf, w_bs, w_out, g_ffn, w_up, conv_w, conv_b,
           w_down, g_ple, w_ple_gate, w_ple_proj, tiles):
    S, D = x.shape
    f0 = 3 * W_ATT
    sb0 = f0 + N_HEADS
    w_t = w_in.T.astype(BF16)
    b_f_row = jnp.pad(b_f[None, :], ((0, 0), (0, LANES - N_HEADS)))

    h, log_f = _norm(x, g_mix[None, :], w_t=w_t, f_row0=f0, b_f=b_f_row, **tiles["norm"])
    proj_a = _proj(_proj_fox_kernel, "proj_fox", h, w_t, 0, f0, (g_q[None, :], g_k[None, :]),
                   **tiles["proj"])
    proj_b = _proj(_proj_sb_kernel, "proj_sb", h, w_t, sb0, w_t.shape[0] - sb0, (),
                   **tiles["proj"])
    c = _cumsum(log_f, **tiles["cumsum"])
    tk = tiles["fox"]["tk"]
    c_rows = c[:, :N_HEADS].T.reshape(N_HEADS, S // tk, 1, tk)
    c_edges = c_rows[:, :, 0, tk - 1]
    qk_max = HEAD_DIM ** 0.5 * LOG2E * QK_SLACK * jnp.max(jnp.abs(g_q)) * jnp.max(jnp.abs(g_k))
    lim = ((UNDERFLOW + C_SLACK) * LOG2E + 2.0 * qk_max).reshape(1)
    y_a = _fox_attn(proj_a, c_rows, c_edges, lim, **tiles["fox"])
    y_b = _sb_attn(proj_b, **tiles["sb"])
    merged = _merge(y_a, y_b, w_bf.astype(BF16), w_bs.astype(BF16), proj_b, **tiles["merge"])
    x = _matmul_res(merged, w_out.astype(BF16), x, **tiles["out"])
    h = _norm(x, g_ffn[None, :], **tiles["norm"])
    gated = _up_glu(h, w_up, conv_w, conv_b[None, :], **tiles["up"])
    x = _down_res(gated, w_down.astype(BF16), x, **tiles["down"])
    h = _norm(x, g_ple[None, :], **tiles["norm"])
    x = _ple(h, w_ple_gate.astype(BF16), p, w_ple_proj.astype(BF16), x, **tiles["ple"])
    return x


def _forward(x, p, g_mix, w_in, b_f, g_q_fox, g_k_fox, w_branch_fox, w_branch_sb, w_out, g_ffn,
             w_up, conv_w, conv_b, w_down, g_ple, w_ple_gate, w_ple_proj, tiles):
    B, S, D = x.shape
    outs = []
    for b in range(B):
        xb = x[b]
        for i in range(w_in.shape[0]):
            xb = _layer(xb, p[i, b], g_mix[i], w_in[i], b_f[i], g_q_fox[i], g_k_fox[i],
                        w_branch_fox[i], w_branch_sb[i], w_out[i], g_ffn[i], w_up[i], conv_w[i],
                        conv_b[i], w_down[i], g_ple[i], w_ple_gate[i], w_ple_proj[i], tiles)
        outs.append(xb)
    return jnp.stack(outs, axis=0)


def kernel(x, p, g_mix, w_in, b_f, g_q_fox, g_k_fox, w_branch_fox, w_branch_sb, w_out, g_ffn,
           w_up, conv_w, conv_b, w_down, g_ple, w_ple_gate, w_ple_proj):
    return _forward(x, p, g_mix, w_in, b_f, g_q_fox, g_k_fox, w_branch_fox, w_branch_sb, w_out,
                    g_ffn, w_up, conv_w, conv_b, w_down, g_ple, w_ple_gate, w_ple_proj,
                    _tiles(x.shape[1]))
```

```python
import functools

import jax
import jax.numpy as jnp
from jax import lax
from jax.experimental import pallas as pl
from jax.experimental.pallas import tpu as pltpu

F32 = jnp.float32
BF16 = jnp.bfloat16

EPS = 1e-6
HEAD_DIM = 128
N_HEADS = 16
W_ATT = N_HEADS * HEAD_DIM
LANES = 128
SUBLANES = 8
MXU_COLS = 256
CONV_WIDTH = 3
VMEM_LIMIT_BYTES = 56 * 1024 * 1024
UNDERFLOW = 105.0
LOG2E = 1.4426950408889634
C_SLACK = 2.0
QK_SLACK = 1.02

COL_QA, COL_KA, COL_VA = 0, 16, 32
COL_QB, COL_KB, COL_VB, COL_GA, COL_GB = 0, 16, 32, 48, 80


def _params(*sem):
    return pltpu.CompilerParams(dimension_semantics=sem, vmem_limit_bytes=VMEM_LIMIT_BYTES)


def _log_sigmoid(z):
    return jnp.minimum(z, 0.0) - jnp.log1p(jnp.exp(-jnp.abs(z)))


def _rms_norm_rows(x, g):
    ms = jnp.mean(x * x, axis=-1, keepdims=True)
    return x * lax.rsqrt(ms + EPS) * g


def _dot(a, b):
    return jnp.dot(a, b, preferred_element_type=F32)


def _dot_nt(a, b):
    return lax.dot_general(a, b, (((1,), (1,)), ((), ())), preferred_element_type=F32)


def _col_blocks(width):
    return [slice(s, s + MXU_COLS) for s in range(0, width, MXU_COLS)]


def _side_cast_specs(weights, n_steps, step_of):
    rows = LANES
    while sum(w.shape[0] // rows for w in weights) > n_steps:
        rows *= 2
    in_specs, out_specs, out_shapes, plan = [], [], [], []
    first = 0
    for w in weights:
        assert w.shape[0] % rows == 0, (w.shape, rows)
        nb = w.shape[0] // rows

        def index(*ids, first=first, nb=nb):
            return jnp.clip(step_of(*ids) - first, 0, nb - 1), 0

        in_specs.append(pl.BlockSpec((rows, w.shape[1]), index))
        out_specs.append(pl.BlockSpec((rows, w.shape[1]), index))
        out_shapes.append(jax.ShapeDtypeStruct(w.shape, BF16))
        plan.append((first, nb))
        first += nb
    return in_specs, out_specs, out_shapes, tuple(plan)


def _side_cast(step, plan, in_refs, out_refs):
    for (first, nb), i_ref, o_ref in zip(plan, in_refs, out_refs):
        @pl.when((step >= first) & (step < first + nb))
        def _():
            o_ref[...] = i_ref[...].astype(BF16)


def _norm_kernel(x_ref, g_ref, o_ref):
    o_ref[...] = _rms_norm_rows(x_ref[...], g_ref[...]).astype(BF16)


def _norm_gate_kernel(x_ref, g_ref, wf_ref, bf_ref, o_ref, lf_ref):
    h = _rms_norm_rows(x_ref[...], g_ref[...]).astype(BF16)
    o_ref[...] = h
    lf_ref[...] = _log_sigmoid(_dot_nt(h, wf_ref[...]) + bf_ref[...]) * LOG2E


def _norm(x, g, *, tr, w_t=None, f_row0=None, b_f=None):
    S, D = x.shape
    row = pl.BlockSpec((tr, D), lambda i: (i, 0))
    gain = pl.BlockSpec((1, D), lambda i: (0, 0))
    if w_t is None:
        return pl.pallas_call(
            _norm_kernel, grid=(S // tr,), in_specs=[row, gain], out_specs=row,
            out_shape=jax.ShapeDtypeStruct((S, D), BF16),
            compiler_params=_params("parallel"), name="norm",
        )(x, g)
    return pl.pallas_call(
        _norm_gate_kernel, grid=(S // tr,),
        in_specs=[row, gain, pl.BlockSpec((LANES, D), lambda i: (f_row0 // LANES, 0)),
                  pl.BlockSpec((1, LANES), lambda i: (0, 0))],
        out_specs=[row, pl.BlockSpec((tr, LANES), lambda i: (i, 0))],
        out_shape=[jax.ShapeDtypeStruct((S, D), BF16), jax.ShapeDtypeStruct((S, LANES), F32)],
        compiler_params=_params("parallel"), name="norm_gate",
    )(x, g, w_t, b_f)


def _proj_sweep(h_ref, w_ref, o_ref, epilogue):
    h = h_ref[...]
    for sl in _col_blocks(w_ref.shape[0]):
        o_ref[:, sl] = epilogue(_dot_nt(h, w_ref[sl, :])).astype(BF16)


def _proj_fox_kernel(h_ref, w_ref, gq_ref, gk_ref, o_ref, *, tn):
    j = pl.program_id(1)
    tiles = W_ATT // tn

    def head_norm(g_ref, mult):
        def epilogue(acc):
            heads = [_rms_norm_rows(acc[:, c:c + HEAD_DIM], g_ref[...]) * mult
                     for c in range(0, acc.shape[1], HEAD_DIM)]
            return jnp.concatenate(heads, axis=1)
        return epilogue

    @pl.when(j < tiles)
    def _():
        _proj_sweep(h_ref, w_ref, o_ref, head_norm(gq_ref, HEAD_DIM ** -0.5 * LOG2E))

    @pl.when((j >= tiles) & (j < 2 * tiles))
    def _():
        _proj_sweep(h_ref, w_ref, o_ref, head_norm(gk_ref, 1.0))

    @pl.when(j >= 2 * tiles)
    def _():
        _proj_sweep(h_ref, w_ref, o_ref, lambda acc: acc)


def _proj_sb_kernel(h_ref, w_ref, o_ref, *, tn):
    j = pl.program_id(1)
    tiles = W_ATT // tn

    @pl.when(j < tiles)
    def _():
        _proj_sweep(h_ref, w_ref, o_ref, lambda acc: acc * (HEAD_DIM ** -0.5 * LOG2E))

    @pl.when((j >= tiles) & (j < 3 * tiles))
    def _():
        _proj_sweep(h_ref, w_ref, o_ref, lambda acc: acc)

    @pl.when(j >= 3 * tiles)
    def _():
        _proj_sweep(h_ref, w_ref, o_ref, jax.nn.sigmoid)


def _proj(kernel, name, h, w_t, row0, n, gains, *, tm, tn):
    S, D = h.shape
    return pl.pallas_call(
        functools.partial(kernel, tn=tn),
        grid=(S // tm, n // tn),
        in_specs=[pl.BlockSpec((tm, D), lambda i, j: (i, 0)),
                  pl.BlockSpec((pl.Element(tn), pl.Element(D)),
                               lambda i, j: (pl.multiple_of(row0 + j * tn, N_HEADS), 0))]
                 + [pl.BlockSpec((1, HEAD_DIM), lambda i, j: (0, 0)) for _ in gains],
        out_specs=pl.BlockSpec((tm, tn), lambda i, j: (i, j)),
        out_shape=jax.ShapeDtypeStruct((S, n), BF16),
        compiler_params=_params("parallel", "arbitrary"),
        name=name,
    )(h, w_t, *gains)


def _cumsum_kernel(x_ref, o_ref, carry_ref, *, tb):
    @pl.when(pl.program_id(0) == 0)
    def _():
        carry_ref[...] = jnp.zeros_like(carry_ref)

    x = x_ref[...]
    r = lax.broadcasted_iota(jnp.int32, (tb, tb), 0)
    c = lax.broadcasted_iota(jnp.int32, (tb, tb), 1)
    tri = (c <= r).astype(BF16)
    x1 = x.astype(BF16)
    r1 = x - x1.astype(F32)
    x2 = r1.astype(BF16)
    x3 = (r1 - x2.astype(F32)).astype(BF16)
    cs = _dot(tri, x1) + _dot(tri, x2) + _dot(tri, x3) + carry_ref[...]
    o_ref[...] = cs
    carry_ref[...] = cs[tb - 1:tb, :]


def _cumsum(x, *, tb):
    S, W = x.shape
    return pl.pallas_call(
        functools.partial(_cumsum_kernel, tb=tb),
        grid=(S // tb,),
        in_specs=[pl.BlockSpec((tb, W), lambda i: (i, 0))],
        out_specs=pl.BlockSpec((tb, W), lambda i: (i, 0)),
        out_shape=jax.ShapeDtypeStruct((S, W), F32),
        scratch_shapes=[pltpu.VMEM((1, W), F32)],
        compiler_params=_params("arbitrary"),
        name="cumsum_logf",
    )(x)


def _fox_kernel(edge_ref, lim_ref, q_ref, k_ref, v_ref, c_ref, *refs, tq, tk, hp, tb, side):
    side_in, (o_ref, *side_out), (m_ref, l_ref, acc_ref) = (
        refs[:len(side)], refs[len(side):2 * len(side) + 1], refs[2 * len(side) + 1:])
    g = pl.program_id(0)
    i = pl.program_id(1)
    _side_cast(g * pl.num_programs(1) + i, side, side_in, side_out)
    nd = tq // tk
    rep = tk // LANES
    m_ref[...] = jnp.full_like(m_ref, -1e30)
    l_ref[...] = jnp.zeros_like(l_ref)
    acc_ref[...] = jnp.zeros_like(acc_ref)
    row = i * tq + lax.broadcasted_iota(jnp.int32, (tb, tk), 0)
    col = lax.broadcasted_iota(jnp.int32, (tb, tk), 1)

    def step(kb, masked):
        start = pl.multiple_of(kb * tk, tk)
        for hh in range(hp):
            sl = slice(hh * HEAD_DIM, (hh + 1) * HEAD_DIM)
            s_all = _dot_nt(q_ref[:, sl], k_ref[pl.ds(start, tk), sl])
            c_row = c_ref[hh, kb]
            p_blocks = []
            for r in range(0, tq, tb):
                rows = slice(r, r + tb)
                s = s_all[rows, :] - c_row
                if masked:
                    s = jnp.where(start + col <= row + r, s, -jnp.inf)
                m_old = m_ref[hh, rows, :]
                m_new = jnp.maximum(m_old, jnp.max(s, axis=-1, keepdims=True))
                alpha = jnp.exp2(m_old - m_new)
                p = jnp.exp2(s - jnp.tile(m_new, (1, rep)))
                l_ref[hh, rows, :] = alpha * l_ref[hh, rows, :] + jnp.sum(p, axis=-1, keepdims=True)
                acc_ref[hh, rows, :] = alpha * acc_ref[hh, rows, :]
                m_ref[hh, rows, :] = m_new
                p_blocks.append(p.astype(BF16))
            p_all = jnp.concatenate(p_blocks, axis=0)
            acc_ref[hh] += _dot(p_all, v_ref[pl.ds(start, tk), sl])

    for d in range(nd):
        step((i + 1) * nd - 1 - d, True)

    top = i * nd - 1
    lim = lim_ref[0]

    def chunks_needed(h):
        e_hi = edge_ref[h, jnp.maximum(top, 0)]

        def needed(kb):
            return (kb >= 0) & (e_hi - edge_ref[h, jnp.maximum(kb, 0)] > -lim)

        return top - lax.while_loop(needed, lambda kb: kb - 1, top)

    n = chunks_needed(g * hp)
    for hh in range(1, hp):
        n = jnp.maximum(n, chunks_needed(g * hp + hh))

    def body(t, carry):
        step(top - t, False)
        return carry

    lax.fori_loop(0, n, body, 0)
    for hh in range(hp):
        sl = slice(hh * HEAD_DIM, (hh + 1) * HEAD_DIM)
        o_ref[:, sl] = (acc_ref[hh] / l_ref[hh]).astype(BF16)


def _fox_attn(proj, c_rows, c_edges, lim, cast_weights, *, tq, tk, hp, tb):
    S = proj.shape[0]
    nkb = S // tk
    w = hp * HEAD_DIM
    ni = S // tq
    side_in, side_out, side_shapes, side = _side_cast_specs(
        cast_weights, (N_HEADS // hp) * ni, lambda g, i: g * ni + i)
    return pl.pallas_call(
        functools.partial(_fox_kernel, tq=tq, tk=tk, hp=hp, tb=tb, side=side),
        grid=(N_HEADS // hp, ni),
        in_specs=[
            pl.BlockSpec(memory_space=pltpu.SMEM),
            pl.BlockSpec(memory_space=pltpu.SMEM),
            pl.BlockSpec((tq, w), lambda g, i: (i, COL_QA // hp + g)),
            pl.BlockSpec((S, w), lambda g, i: (0, COL_KA // hp + g)),
            pl.BlockSpec((S, w), lambda g, i: (0, COL_VA // hp + g)),
            pl.BlockSpec((hp, nkb, 1, tk), lambda g, i: (g, 0, 0, 0)),
        ] + side_in,
        out_specs=[pl.BlockSpec((tq, w), lambda g, i: (i, g))] + side_out,
        out_shape=[jax.ShapeDtypeStruct((S, W_ATT), BF16)] + side_shapes,
        scratch_shapes=[
            pltpu.VMEM((hp, tq, LANES), F32),
            pltpu.VMEM((hp, tq, LANES), F32),
            pltpu.VMEM((hp, tq, HEAD_DIM), F32),
        ],
        compiler_params=_params("arbitrary", "arbitrary"),
        name="fox_attn",
    )(c_edges, lim, proj, proj, proj, c_rows, *cast_weights)


def _sb_kernel(q_ref, k_ref, v_ref, o_ref, run_ref, acc_ref, *, tq, tk, hp, tb):
    i = pl.program_id(1)
    nd = tq // tk
    rep = tk // LANES
    run_ref[...] = jnp.zeros_like(run_ref)
    acc_ref[...] = jnp.zeros_like(acc_ref)
    row = i * tq + lax.broadcasted_iota(jnp.int32, (tb, tk), 0)
    col = lax.broadcasted_iota(jnp.int32, (tb, tk), 1)
    jj = lax.broadcasted_iota(jnp.int32, (tk, tk), 0)
    ss = lax.broadcasted_iota(jnp.int32, (tk, tk), 1)
    later = (jj > ss).astype(BF16)

    def chunk(kb, masked):
        start = pl.multiple_of(kb * tk, tk)
        run_top = None
        for hh in range(hp):
            sl = slice(hh * HEAD_DIM, (hh + 1) * HEAD_DIM)
            z_all = _dot_nt(q_ref[:, sl], k_ref[pl.ds(start, tk), sl])
            log_betas, his, los, row_sums = [], [], [], []
            for r in range(0, tq, tb):
                z = z_all[r:r + tb, :]
                log_beta = jnp.minimum(z, 0.0) - jnp.log2(1.0 + jnp.exp2(-jnp.abs(z)))
                l1m = log_beta - z
                if masked:
                    l1m = jnp.where(start + col < row + r, l1m, 0.0)
                hi = l1m.astype(BF16)
                row_sums.append(jnp.sum(l1m, axis=-1, keepdims=True))
                log_betas.append(log_beta)
                his.append(hi)
                los.append((l1m - hi.astype(F32)).astype(BF16))
            tail_all = (_dot(jnp.concatenate(his, axis=0), later)
                        + _dot(jnp.concatenate(los, axis=0), later))
            a_blocks = []
            for b, r in enumerate(range(0, tq, tb)):
                rows = slice(r, r + tb)
                run = run_ref[hh, rows, :]
                tail = tail_all[rows, :]
                a = jnp.exp2(log_betas[b] + tail + jnp.tile(run, (1, rep)))
                if masked:
                    a = jnp.where(start + col < row + r, a, 0.0)
                a_blocks.append(a.astype(BF16))
                run_new = run + row_sums[b]
                run_ref[hh, rows, :] = run_new
                run_top = run_new if run_top is None else jnp.maximum(run_top, run_new)
            acc_ref[hh] += _dot(jnp.concatenate(a_blocks, axis=0), v_ref[pl.ds(start, tk), sl])
        return jnp.max(run_top)

    run_max = None
    for d in range(nd):
        run_max = chunk((i + 1) * nd - 1 - d, True)

    def cond(carry):
        kb, run_max = carry
        return (kb >= 0) & (run_max > -UNDERFLOW * LOG2E)

    def body(carry):
        kb, _ = carry
        return kb - 1, chunk(kb, False)

    lax.while_loop(cond, body, (i * nd - 1, run_max))
    for hh in range(hp):
        sl = slice(hh * HEAD_DIM, (hh + 1) * HEAD_DIM)
        o_ref[:, sl] = acc_ref[hh].astype(BF16)


def _sb_attn(proj, *, tq, tk, hp, tb):
    S = proj.shape[0]
    w = hp * HEAD_DIM
    return pl.pallas_call(
        functools.partial(_sb_kernel, tq=tq, tk=tk, hp=hp, tb=tb),
        grid=(N_HEADS // hp, S // tq),
        in_specs=[
            pl.BlockSpec((tq, w), lambda g, i: (i, COL_QB // hp + g)),
            pl.BlockSpec((S, w), lambda g, i: (0, COL_KB // hp + g)),
            pl.BlockSpec((S, w), lambda g, i: (0, COL_VB // hp + g)),
        ],
        out_specs=pl.BlockSpec((tq, w), lambda g, i: (i, g)),
        out_shape=jax.ShapeDtypeStruct((S, W_ATT), BF16),
        scratch_shapes=[
            pltpu.VMEM((hp, tq, LANES), F32),
            pltpu.VMEM((hp, tq, HEAD_DIM), F32),
        ],
        compiler_params=_params("parallel", "arbitrary"),
        name="sb_attn",
    )(proj, proj, proj)


def _merge_kernel(ya_ref, yb_ref, wa_ref, wb_ref, ga_ref, gb_ref, *refs, side):
    side_in, (o_ref, *side_out) = refs[:len(side)], refs[len(side):]
    _side_cast(pl.program_id(0) * pl.num_programs(1) + pl.program_id(1), side, side_in, side_out)
    ya = ya_ref[...]
    yb = yb_ref[...]
    for sl in _col_blocks(o_ref.shape[1]):
        a = _dot(ya, wa_ref[:, sl])
        b = _dot(yb, wb_ref[:, sl])
        o_ref[:, sl] = (ga_ref[:, sl].astype(F32) * a + gb_ref[:, sl].astype(F32) * b).astype(BF16)


def _merge(y_a, y_b, w_a, w_b, proj, cast_weights, *, tm, tn):
    S, K = y_a.shape
    N = w_a.shape[1]
    ga0 = COL_GA * LANES // tn
    gb0 = COL_GB * LANES // tn
    nj = N // tn
    side_in, side_out, side_shapes, side = _side_cast_specs(
        cast_weights, (S // tm) * nj, lambda i, j: i * nj + j)
    return pl.pallas_call(
        functools.partial(_merge_kernel, side=side),
        grid=(S // tm, nj),
        in_specs=[
            pl.BlockSpec((tm, K), lambda i, j: (i, 0)),
            pl.BlockSpec((tm, K), lambda i, j: (i, 0)),
            pl.BlockSpec((K, tn), lambda i, j: (0, j)),
            pl.BlockSpec((K, tn), lambda i, j: (0, j)),
            pl.BlockSpec((tm, tn), lambda i, j: (i, ga0 + j)),
            pl.BlockSpec((tm, tn), lambda i, j: (i, gb0 + j)),
        ] + side_in,
        out_specs=[pl.BlockSpec((tm, tn), lambda i, j: (i, j))] + side_out,
        out_shape=[jax.ShapeDtypeStruct((S, N), BF16)] + side_shapes,
        compiler_params=_params("arbitrary", "arbitrary"),
        name="merge",
    )(y_a, y_b, w_a, w_b, proj, proj, *cast_weights)


def _matmul_res_kernel(a_ref, w_ref, x_ref, *refs, side):
    side_in, (o_ref, *side_out) = refs[:len(side)], refs[len(side):]
    _side_cast(pl.program_id(0) * pl.num_programs(1) + pl.program_id(1), side, side_in, side_out)
    a = a_ref[...]
    for sl in _col_blocks(o_ref.shape[1]):
        o_ref[:, sl] = x_ref[:, sl] + _dot(a, w_ref[:, sl])


def _matmul_res(a, w, x, cast_weights, *, tm, tn):
    S, K = a.shape
    N = w.shape[1]
    nj = N // tn
    side_in, side_out, side_shapes, side = _side_cast_specs(
        cast_weights, (S // tm) * nj, lambda i, j: i * nj + j)
    return pl.pallas_call(
        functools.partial(_matmul_res_kernel, side=side),
        grid=(S // tm, nj),
        in_specs=[
            pl.BlockSpec((tm, K), lambda i, j: (i, 0)),
            pl.BlockSpec((K, tn), lambda i, j: (0, j)),
            pl.BlockSpec((tm, tn), lambda i, j: (i, j)),
        ] + side_in,
        out_specs=[pl.BlockSpec((tm, tn), lambda i, j: (i, j))] + side_out,
        out_shape=[jax.ShapeDtypeStruct((S, N), F32)] + side_shapes,
        compiler_params=_params("arbitrary", "arbitrary"),
        name="out_res",
    )(a, w, x, *cast_weights)


def _up_glu_kernel(h_ref, wg32_ref, wv32_ref, cwg_ref, cwv_ref, cbg_ref, cbv_ref, o_ref,
                   wg0_ref, wg1_ref, wv0_ref, wv1_ref, ug0_ref, ug1_ref, uv0_ref, uv1_ref,
                   carry_g_ref, carry_v_ref, *, tm, ts, nj, n_tiles):
    t = pl.program_id(0)
    tile = jnp.clip(t - 1, 0, n_tiles - 1)
    j = tile % nj
    wg_refs, wv_refs = (wg0_ref, wg1_ref), (wv0_ref, wv1_ref)
    ug_refs, uv_refs = (ug0_ref, ug1_ref), (uv0_ref, uv1_ref)
    tc = min(ts, 4 * SUBLANES)
    kc = h_ref.shape[1] * tc // ts
    halo = SUBLANES
    first = tile // nj == 0

    @pl.when(t == 0)
    def _():
        wg1_ref[...] = jnp.zeros_like(wg1_ref)
        wv1_ref[...] = jnp.zeros_like(wv1_ref)
        ug0_ref[...] = jnp.zeros_like(ug0_ref)
        uv0_ref[...] = jnp.zeros_like(uv0_ref)

    def cast_chunk(slot, ks):
        wg_refs[slot][ks, :] = wg32_ref[ks, :].astype(BF16)
        wv_refs[slot][ks, :] = wv32_ref[ks, :].astype(BF16)

    def epilogue_chunk(slot, r):
        def conv(u_ref, cw_ref, cb_ref):
            cw = cw_ref[...]
            rows = lambda back: u_ref[halo - back + r:halo - back + r + tc, :]
            return cb_ref[...] + cw[0:1, :] * rows(2) + cw[1:2, :] * rows(1) + cw[2:3, :] * rows(0)

        gate = conv(ug_refs[slot], cwg_ref, cbg_ref)
        val = conv(uv_refs[slot], cwv_ref, cbv_ref)
        o_ref[r:r + tc, :] = (gate * jax.nn.sigmoid(gate) * val).astype(BF16)

    def rows_step(p, r0):
        acc_g = acc_v = None
        for c in range(ts // tc):
            ks = slice(c * kc, (c + 1) * kc)
            lhs = h_ref[r0:r0 + ts, ks]
            dg = _dot(lhs, wg_refs[1 - p][ks, :])
            dv = _dot(lhs, wv_refs[1 - p][ks, :])
            acc_g = dg if acc_g is None else acc_g + dg
            acc_v = dv if acc_v is None else acc_v + dv
            if r0 == 0:
                cast_chunk(p, ks)
            epilogue_chunk(p, r0 + c * tc)
        for u, u_ref, carry_ref in ((acc_g, ug_refs[1 - p], carry_g_ref),
                                    (acc_v, uv_refs[1 - p], carry_v_ref)):
            if r0 == 0:
                u_ref[0:halo, :] = jnp.where(first, 0.0, carry_ref[j])
            u_ref[halo + r0:halo + r0 + ts, :] = u
            if r0 + ts == tm:
                carry_ref[j] = u[ts - halo:ts, :]

    for parity in (0, 1):
        @pl.when(t % 2 == parity)
        def _():
            for r0 in range(0, tm, ts):
                rows_step(parity, r0)


def _up_glu(h, w_up, conv_w, conv_b, *, tm, tn, ts):
    S, D = h.shape
    d_ff = w_up.shape[1] // 2
    nj = d_ff // tn
    n_tiles = (S // tm) * nj
    cast = lambda t: jnp.minimum(t, n_tiles - 1)
    mult = lambda t: jnp.clip(t - 1, 0, n_tiles - 1)
    done = lambda t: jnp.clip(t - 2, 0, n_tiles - 1)
    wbuf = pltpu.VMEM((D, tn), BF16)
    ubuf = pltpu.VMEM((tm + SUBLANES, tn), F32)
    return pl.pallas_call(
        functools.partial(_up_glu_kernel, tm=tm, ts=ts, nj=nj, n_tiles=n_tiles),
        grid=(n_tiles + 2,),
        in_specs=[
            pl.BlockSpec((tm, D), lambda t: (mult(t) // nj, 0), pipeline_mode=pl.Buffered(1)),
            pl.BlockSpec((D, tn), lambda t: (0, cast(t) % nj)),
            pl.BlockSpec((D, tn), lambda t: (0, nj + cast(t) % nj)),
            pl.BlockSpec((CONV_WIDTH, tn), lambda t: (0, done(t) % nj)),
            pl.BlockSpec((CONV_WIDTH, tn), lambda t: (0, nj + done(t) % nj)),
            pl.BlockSpec((1, tn), lambda t: (0, done(t) % nj)),
            pl.BlockSpec((1, tn), lambda t: (0, nj + done(t) % nj)),
        ],
        out_specs=pl.BlockSpec((tm, tn), lambda t: (done(t) // nj, done(t) % nj)),
        out_shape=jax.ShapeDtypeStruct((S, d_ff), BF16),
        scratch_shapes=[wbuf, wbuf, wbuf, wbuf, ubuf, ubuf, ubuf, ubuf,
                        pltpu.VMEM((nj, SUBLANES, tn), F32),
                        pltpu.VMEM((nj, SUBLANES, tn), F32)],
        compiler_params=_params("arbitrary"),
        name="up_glu",
    )(h, w_up, w_up, conv_w, conv_w, conv_b, conv_b)


def _down_res_kernel(a_ref, w_ref, x_ref, o_ref, acc_ref):
    k = pl.program_id(2)

    @pl.when(k == 0)
    def _():
        acc_ref[...] = x_ref[...]

    acc_ref[...] += _dot(a_ref[...], w_ref[...])

    @pl.when(k == pl.num_programs(2) - 1)
    def _():
        o_ref[...] = acc_ref[...]


def _down_res(a, w, x, *, tm, tn, tk):
    S, K = a.shape
    N = w.shape[1]
    return pl.pallas_call(
        _down_res_kernel,
        grid=(S // tm, N // tn, K // tk),
        in_specs=[
            pl.BlockSpec((tm, tk), lambda i, j, k: (i, k)),
            pl.BlockSpec((tk, tn), lambda i, j, k: (k, j)),
            pl.BlockSpec((tm, tn), lambda i, j, k: (i, j)),
        ],
        out_specs=pl.BlockSpec((tm, tn), lambda i, j, k: (i, j)),
        out_shape=jax.ShapeDtypeStruct((S, N), F32),
        scratch_shapes=[pltpu.VMEM((tm, tn), F32)],
        compiler_params=_params("parallel", "parallel", "arbitrary"),
        name="down_res",
    )(a, w, x)


def _ple_kernel(h_ref, wg_ref, p_ref, wp_ref, x_ref, o_ref):
    h = h_ref[...]
    p = p_ref[...].astype(BF16)
    for sl in _col_blocks(o_ref.shape[1]):
        gate = jax.nn.sigmoid(_dot(h, wg_ref[:, sl]))
        o_ref[:, sl] = x_ref[:, sl] + gate * _dot(p, wp_ref[:, sl])


def _ple(h, w_gate, p, w_proj, x, *, tm, tn):
    S, D = h.shape
    P = p.shape[1]
    return pl.pallas_call(
        _ple_kernel,
        grid=(S // tm, D // tn),
        in_specs=[
            pl.BlockSpec((tm, D), lambda i, j: (i, 0)),
            pl.BlockSpec((D, tn), lambda i, j: (0, j)),
            pl.BlockSpec((tm, P), lambda i, j: (i, 0)),
            pl.BlockSpec((P, tn), lambda i, j: (0, j)),
            pl.BlockSpec((tm, tn), lambda i, j: (i, j)),
        ],
        out_specs=pl.BlockSpec((tm, tn), lambda i, j: (i, j)),
        out_shape=jax.ShapeDtypeStruct((S, D), F32),
        compiler_params=_params("parallel", "arbitrary"),
        name="ple",
    )(h, w_gate, p, w_proj, x)


def _tiles(S):
    return dict(
        norm=dict(tr=min(256, S)),
        proj=dict(tm=min(1024, S), tn=1024),
        cumsum=dict(tb=min(512, S)),
        fox=dict(tq=min(512, S), tk=min(512, S), hp=2, tb=32),
        sb=dict(tq=min(256, S), tk=min(256, S), hp=4, tb=32),
        merge=dict(tm=min(1024, S), tn=512),
        out=dict(tm=min(1024, S), tn=512),
        up=dict(tm=min(2048, S), tn=256, ts=min(512, S)),
        down=dict(tm=min(1024, S), tn=512, tk=5504),
        ple=dict(tm=min(1024, S), tn=512),
    )


def _layer(x, p, g_mix, w_in, b_f, g_q, g_k, w_bf, w_bs, w_out, g_ffn, w_up, conv_w, conv_b,
           w_down, g_ple, w_ple_gate, w_ple_proj, tiles):
    S, D = x.shape
    f0 = 3 * W_ATT
    sb0 = f0 + N_HEADS
    w_t = w_in.T.astype(BF16)
    b_f_row = jnp.pad(b_f[None, :], ((0, 0), (0, LANES - N_HEADS)))

    h, log_f = _norm(x, g_mix[None, :], w_t=w_t, f_row0=f0, b_f=b_f_row, **tiles["norm"])
    proj_a = _proj(_proj_fox_kernel, "proj_fox", h, w_t, 0, f0, (g_q[None, :], g_k[None, :]),
                   **tiles["proj"])
    proj_b = _proj(_proj_sb_kernel, "proj_sb", h, w_t, sb0, w_t.shape[0] - sb0, (),
                   **tiles["proj"])
    c = _cumsum(log_f, **tiles["cumsum"])
    tk = tiles["fox"]["tk"]
    c_rows = c[:, :N_HEADS].T.reshape(N_HEADS, S // tk, 1, tk)
    c_edges = c_rows[:, :, 0, tk - 1]
    qk_max = HEAD_DIM ** 0.5 * LOG2E * QK_SLACK * jnp.max(jnp.abs(g_q)) * jnp.max(jnp.abs(g_k))
    lim = ((UNDERFLOW + C_SLACK) * LOG2E + 2.0 * qk_max).reshape(1)
    y_a, w_bf16, w_bs16, w_out16 = _fox_attn(proj_a, c_rows, c_edges, lim, [w_bf, w_bs, w_out],
                                             **tiles["fox"])
    y_b = _sb_attn(proj_b, **tiles["sb"])
    merged, w_down16 = _merge(y_a, y_b, w_bf16, w_bs16, proj_b, [w_down], **tiles["merge"])
    x, w_gate16 = _matmul_res(merged, w_out16, x, [w_ple_gate], **tiles["out"])
    h = _norm(x, g_ffn[None, :], **tiles["norm"])
    gated = _up_glu(h, w_up, conv_w, conv_b[None, :], **tiles["up"])
    x = _down_res(gated, w_down16, x, **tiles["down"])
    h = _norm(x, g_ple[None, :], **tiles["norm"])
    x = _ple(h, w_gate16, p, w_ple_proj.astype(BF16), x, **tiles["ple"])
    return x


def _forward(x, p, g_mix, w_in, b_f, g_q_fox, g_k_fox, w_branch_fox, w_branch_sb, w_out, g_ffn,
             w_up, conv_w, conv_b, w_down, g_ple, w_ple_gate, w_ple_proj, tiles):
    B, S, D = x.shape
    outs = []
    for b in range(B):
        xb = x[b]
        for i in range(w_in.shape[0]):
            xb = _layer(xb, p[i, b], g_mix[i], w_in[i], b_f[i], g_q_fox[i], g_k_fox[i],
                        w_branch_fox[i], w_branch_sb[i], w_out[i], g_ffn[i], w_up[i], conv_w[i],
                        conv_b[i], w_down[i], g_ple[i], w_ple_gate[i], w_ple_proj[i], tiles)
        outs.append(xb)
    return jnp.stack(outs, axis=0)


def kernel(x, p, g_mix, w_in, b_f, g_q_fox, g_k_fox, w_branch_fox, w_branch_sb, w_out, g_ffn,
           w_up, conv_w, conv_b, w_down, g_ple, w_ple_gate, w_ple_proj):
    return _forward(x, p, g_mix, w_in, b_f, g_q_fox, g_k_fox, w_branch_fox, w_branch_sb, w_out,
                    g_ffn, w_up, conv_w, conv_b, w_down, g_ple, w_ple_gate, w_ple_proj,
                    _tiles(x.shape[1]))
```

```python
import functools

import jax
import jax.numpy as jnp
from jax import lax
from jax.experimental import pallas as pl
from jax.experimental.pallas import tpu as pltpu

F32 = jnp.float32
BF16 = jnp.bfloat16

EPS = 1e-6
HEAD_DIM = 128
N_HEADS = 16
W_ATT = N_HEADS * HEAD_DIM
LANES = 128
SUBLANES = 8
MXU_COLS = 256
CONV_WIDTH = 3
VMEM_LIMIT_BYTES = 56 * 1024 * 1024
UNDERFLOW = 105.0
LOG2E = 1.4426950408889634
C_SLACK = 2.0
QK_SLACK = 1.02

COL_QA, COL_KA, COL_VA = 0, 16, 32
COL_QB, COL_KB, COL_VB, COL_GA, COL_GB = 0, 16, 32, 48, 80


def _params(*sem):
    return pltpu.CompilerParams(dimension_semantics=sem, vmem_limit_bytes=VMEM_LIMIT_BYTES)


def _log_sigmoid(z):
    return jnp.minimum(z, 0.0) - jnp.log1p(jnp.exp(-jnp.abs(z)))


def _rms_norm_rows(x, g):
    ms = jnp.mean(x * x, axis=-1, keepdims=True)
    return x * lax.rsqrt(ms + EPS) * g


def _dot(a, b):
    return jnp.dot(a, b, preferred_element_type=F32)


def _dot_nt(a, b):
    return lax.dot_general(a, b, (((1,), (1,)), ((), ())), preferred_element_type=F32)


def _col_blocks(width):
    return [slice(s, s + MXU_COLS) for s in range(0, width, MXU_COLS)]


def _side_cast_specs(weights, n_steps, step_of):
    rows = LANES
    while sum(w.shape[0] // rows for w in weights) > n_steps:
        rows *= 2
    in_specs, out_specs, out_shapes, plan = [], [], [], []
    first = 0
    for w in weights:
        assert w.shape[0] % rows == 0, (w.shape, rows)
        nb = w.shape[0] // rows

        def index(*ids, first=first, nb=nb):
            return jnp.clip(step_of(*ids) - first, 0, nb - 1), 0

        in_specs.append(pl.BlockSpec((rows, w.shape[1]), index))
        out_specs.append(pl.BlockSpec((rows, w.shape[1]), index))
        out_shapes.append(jax.ShapeDtypeStruct(w.shape, BF16))
        plan.append((first, nb))
        first += nb
    return in_specs, out_specs, out_shapes, tuple(plan)


def _side_cast(step, plan, in_refs, out_refs):
    for (first, nb), i_ref, o_ref in zip(plan, in_refs, out_refs):
        @pl.when((step >= first) & (step < first + nb))
        def _():
            o_ref[...] = i_ref[...].astype(BF16)


def _norm_kernel(x_ref, g_ref, o_ref):
    o_ref[...] = _rms_norm_rows(x_ref[...], g_ref[...]).astype(BF16)


def _norm_gate_kernel(x_ref, g_ref, wf_ref, bf_ref, o_ref, lf_ref):
    h = _rms_norm_rows(x_ref[...], g_ref[...]).astype(BF16)
    o_ref[...] = h
    lf_ref[...] = _log_sigmoid(_dot_nt(h, wf_ref[...]) + bf_ref[...]) * LOG2E


def _norm(x, g, *, tr, w_f=None, b_f=None):
    S, D = x.shape
    row = pl.BlockSpec((tr, D), lambda i: (i, 0))
    gain = pl.BlockSpec((1, D), lambda i: (0, 0))
    if w_f is None:
        return pl.pallas_call(
            _norm_kernel, grid=(S // tr,), in_specs=[row, gain], out_specs=row,
            out_shape=jax.ShapeDtypeStruct((S, D), BF16),
            compiler_params=_params("parallel"), name="norm",
        )(x, g)
    return pl.pallas_call(
        _norm_gate_kernel, grid=(S // tr,),
        in_specs=[row, gain, pl.BlockSpec((LANES, D), lambda i: (0, 0)),
                  pl.BlockSpec((1, LANES), lambda i: (0, 0))],
        out_specs=[row, pl.BlockSpec((tr, LANES), lambda i: (i, 0))],
        out_shape=[jax.ShapeDtypeStruct((S, D), BF16), jax.ShapeDtypeStruct((S, LANES), F32)],
        compiler_params=_params("parallel"), name="norm_gate",
    )(x, g, w_f, b_f)


def _proj_step(h_ref, w32_ref, o_ref, wb_cast_ref, wb_mult_ref, epilogue, *, tm, ts):
    tn, d = w32_ref.shape
    kc = MXU_COLS
    for r0 in range(0, tm, ts):
        accs = [None] * (tn // MXU_COLS)
        for c in range(d // kc):
            ks = slice(c * kc, (c + 1) * kc)
            lhs = h_ref[r0:r0 + ts, ks]
            for s, sl in enumerate(_col_blocks(tn)):
                part = _dot_nt(lhs, wb_mult_ref[sl, ks])
                accs[s] = part if accs[s] is None else accs[s] + part
            if r0 == 0:
                wb_cast_ref[:, ks] = w32_ref[:, ks].astype(BF16)
        for s, sl in enumerate(_col_blocks(tn)):
            o_ref[r0:r0 + ts, sl] = epilogue(accs[s]).astype(BF16)


def _proj_kernel(h_ref, w32_ref, *refs, tm, ts, nj, n_tiles, sections):
    *gain_refs, o_ref, wb0_ref, wb1_ref = refs
    wb_refs = (wb0_ref, wb1_ref)
    t = pl.program_id(0)
    j = jnp.clip(t - 1, 0, n_tiles - 1) % nj

    @pl.when(t == 0)
    def _():
        wb1_ref[...] = jnp.zeros_like(wb1_ref)

    def head_norm(g_ref, mult):
        def epilogue(acc):
            heads = [_rms_norm_rows(acc[:, c:c + HEAD_DIM], g_ref[...]) * mult
                     for c in range(0, acc.shape[1], HEAD_DIM)]
            return jnp.concatenate(heads, axis=1)
        return epilogue

    scale = HEAD_DIM ** -0.5 * LOG2E
    epilogues = {
        "q_norm": lambda: head_norm(gain_refs[0], scale),
        "k_norm": lambda: head_norm(gain_refs[1], 1.0),
        "q_scale": lambda: (lambda acc: acc * scale),
        "plain": lambda: (lambda acc: acc),
        "sigmoid": lambda: jax.nn.sigmoid,
    }
    for lo, hi, kind in sections:
        for parity in (0, 1):
            @pl.when((j >= lo) & (j < hi) & (t % 2 == parity))
            def _():
                _proj_step(h_ref, w32_ref, o_ref, wb_refs[parity], wb_refs[1 - parity],
                           epilogues[kind](), tm=tm, ts=ts)


def _proj(name, h, w_t, row0, sections, gains, *, tm, ts, tn):
    S, D = h.shape
    nj = sum(n for n, _ in sections)
    bounds, lo = [], 0
    for n, kind in sections:
        bounds.append((lo, lo + n, kind))
        lo += n
    n_tiles = (S // tm) * nj
    cast = lambda t: jnp.minimum(t, n_tiles - 1)
    mult = lambda t: jnp.clip(t - 1, 0, n_tiles - 1)
    return pl.pallas_call(
        functools.partial(_proj_kernel, tm=tm, ts=ts, nj=nj, n_tiles=n_tiles,
                          sections=tuple(bounds)),
        grid=(n_tiles + 1,),
        in_specs=[pl.BlockSpec((tm, D), lambda t: (mult(t) // nj, 0), pipeline_mode=pl.Buffered(1)),
                  pl.BlockSpec((pl.Element(tn), pl.Element(D)),
                               lambda t: (pl.multiple_of(row0 + (cast(t) % nj) * tn, N_HEADS), 0))]
                 + [pl.BlockSpec((1, HEAD_DIM), lambda t: (0, 0)) for _ in gains],
        out_specs=pl.BlockSpec((tm, tn), lambda t: (mult(t) // nj, mult(t) % nj)),
        out_shape=jax.ShapeDtypeStruct((S, nj * tn), BF16),
        scratch_shapes=[pltpu.VMEM((tn, D), BF16), pltpu.VMEM((tn, D), BF16)],
        compiler_params=_params("arbitrary"),
        name=name,
    )(h, w_t, *gains)


def _cumsum_kernel(x_ref, o_ref, carry_ref, *, tb):
    @pl.when(pl.program_id(0) == 0)
    def _():
        carry_ref[...] = jnp.zeros_like(carry_ref)

    x = x_ref[...]
    r = lax.broadcasted_iota(jnp.int32, (tb, tb), 0)
    c = lax.broadcasted_iota(jnp.int32, (tb, tb), 1)
    tri = (c <= r).astype(BF16)
    x1 = x.astype(BF16)
    r1 = x - x1.astype(F32)
    x2 = r1.astype(BF16)
    x3 = (r1 - x2.astype(F32)).astype(BF16)
    cs = _dot(tri, x1) + _dot(tri, x2) + _dot(tri, x3) + carry_ref[...]
    o_ref[...] = cs
    carry_ref[...] = cs[tb - 1:tb, :]


def _cumsum(x, *, tb):
    S, W = x.shape
    return pl.pallas_call(
        functools.partial(_cumsum_kernel, tb=tb),
        grid=(S // tb,),
        in_specs=[pl.BlockSpec((tb, W), lambda i: (i, 0))],
        out_specs=pl.BlockSpec((tb, W), lambda i: (i, 0)),
        out_shape=jax.ShapeDtypeStruct((S, W), F32),
        scratch_shapes=[pltpu.VMEM((1, W), F32)],
        compiler_params=_params("arbitrary"),
        name="cumsum_logf",
    )(x)


def _fox_kernel(edge_ref, lim_ref, q_ref, k_ref, v_ref, c_ref, *refs, tq, tk, hp, tb, side):
    side_in, (o_ref, *side_out), (m_ref, l_ref, acc_ref) = (
        refs[:len(side)], refs[len(side):2 * len(side) + 1], refs[2 * len(side) + 1:])
    g = pl.program_id(0)
    i = pl.program_id(1)
    _side_cast(g * pl.num_programs(1) + i, side, side_in, side_out)
    nd = tq // tk
    rep = tk // LANES
    m_ref[...] = jnp.full_like(m_ref, -1e30)
    l_ref[...] = jnp.zeros_like(l_ref)
    acc_ref[...] = jnp.zeros_like(acc_ref)
    row = i * tq + lax.broadcasted_iota(jnp.int32, (tb, tk), 0)
    col = lax.broadcasted_iota(jnp.int32, (tb, tk), 1)

    def step(kb, masked):
        start = pl.multiple_of(kb * tk, tk)
        for hh in range(hp):
            sl = slice(hh * HEAD_DIM, (hh + 1) * HEAD_DIM)
            s_all = _dot_nt(q_ref[:, sl], k_ref[pl.ds(start, tk), sl])
            c_row = c_ref[hh, kb]
            p_blocks = []
            for r in range(0, tq, tb):
                rows = slice(r, r + tb)
                s = s_all[rows, :] - c_row
                if masked:
                    s = jnp.where(start + col <= row + r, s, -jnp.inf)
                m_old = m_ref[hh, rows, :]
                m_new = jnp.maximum(m_old, jnp.max(s, axis=-1, keepdims=True))
                alpha = jnp.exp2(m_old - m_new)
                p = jnp.exp2(s - jnp.tile(m_new, (1, rep)))
                l_ref[hh, rows, :] = alpha * l_ref[hh, rows, :] + jnp.sum(p, axis=-1, keepdims=True)
                acc_ref[hh, rows, :] = alpha * acc_ref[hh, rows, :]
                m_ref[hh, rows, :] = m_new
                p_blocks.append(p.astype(BF16))
            p_all = jnp.concatenate(p_blocks, axis=0)
            acc_ref[hh] += _dot(p_all, v_ref[pl.ds(start, tk), sl])

    for d in range(nd):
        step((i + 1) * nd - 1 - d, True)

    top = i * nd - 1
    lim = lim_ref[0]

    def chunks_needed(h):
        e_hi = edge_ref[h, jnp.maximum(top, 0)]

        def needed(kb):
            return (kb >= 0) & (e_hi - edge_ref[h, jnp.maximum(kb, 0)] > -lim)

        return top - lax.while_loop(needed, lambda kb: kb - 1, top)

    n = chunks_needed(g * hp)
    for hh in range(1, hp):
        n = jnp.maximum(n, chunks_needed(g * hp + hh))

    def body(t, carry):
        step(top - t, False)
        return carry

    lax.fori_loop(0, n, body, 0)
    for hh in range(hp):
        sl = slice(hh * HEAD_DIM, (hh + 1) * HEAD_DIM)
        o_ref[:, sl] = (acc_ref[hh] / l_ref[hh]).astype(BF16)


def _fox_attn(proj, c_rows, c_edges, lim, cast_weights, *, tq, tk, hp, tb):
    S = proj.shape[0]
    nkb = S // tk
    w = hp * HEAD_DIM
    ni = S // tq
    side_in, side_out, side_shapes, side = _side_cast_specs(
        cast_weights, (N_HEADS // hp) * ni, lambda g, i: g * ni + i)
    return pl.pallas_call(
        functools.partial(_fox_kernel, tq=tq, tk=tk, hp=hp, tb=tb, side=side),
        grid=(N_HEADS // hp, ni),
        in_specs=[
            pl.BlockSpec(memory_space=pltpu.SMEM),
            pl.BlockSpec(memory_space=pltpu.SMEM),
            pl.BlockSpec((tq, w), lambda g, i: (i, COL_QA // hp + g)),
            pl.BlockSpec((S, w), lambda g, i: (0, COL_KA // hp + g)),
            pl.BlockSpec((S, w), lambda g, i: (0, COL_VA // hp + g)),
            pl.BlockSpec((hp, nkb, 1, tk), lambda g, i: (g, 0, 0, 0)),
        ] + side_in,
        out_specs=[pl.BlockSpec((tq, w), lambda g, i: (i, g))] + side_out,
        out_shape=[jax.ShapeDtypeStruct((S, W_ATT), BF16)] + side_shapes,
        scratch_shapes=[
            pltpu.VMEM((hp, tq, LANES), F32),
            pltpu.VMEM((hp, tq, LANES), F32),
            pltpu.VMEM((hp, tq, HEAD_DIM), F32),
        ],
        compiler_params=_params("arbitrary", "arbitrary"),
        name="fox_attn",
    )(c_edges, lim, proj, proj, proj, c_rows, *cast_weights)


def _sb_kernel(q_ref, k_ref, v_ref, o_ref, run_ref, acc_ref, *, tq, tk, hp, tb):
    i = pl.program_id(1)
    nd = tq // tk
    rep = tk // LANES
    run_ref[...] = jnp.zeros_like(run_ref)
    acc_ref[...] = jnp.zeros_like(acc_ref)
    row = i * tq + lax.broadcasted_iota(jnp.int32, (tb, tk), 0)
    col = lax.broadcasted_iota(jnp.int32, (tb, tk), 1)
    jj = lax.broadcasted_iota(jnp.int32, (tk, tk), 0)
    ss = lax.broadcasted_iota(jnp.int32, (tk, tk), 1)
    later = (jj > ss).astype(BF16)

    def chunk(kb, masked):
        start = pl.multiple_of(kb * tk, tk)
        run_top = None
        for hh in range(hp):
            sl = slice(hh * HEAD_DIM, (hh + 1) * HEAD_DIM)
            z_all = _dot_nt(q_ref[:, sl], k_ref[pl.ds(start, tk), sl])
            log_betas, his, los, row_sums = [], [], [], []
            for r in range(0, tq, tb):
                z = z_all[r:r + tb, :]
                log_beta = jnp.minimum(z, 0.0) - jnp.log2(1.0 + jnp.exp2(-jnp.abs(z)))
                l1m = log_beta - z
                if masked:
                    l1m = jnp.where(start + col < row + r, l1m, 0.0)
                hi = l1m.astype(BF16)
                row_sums.append(jnp.sum(l1m, axis=-1, keepdims=True))
                log_betas.append(log_beta)
                his.append(hi)
                los.append((l1m - hi.astype(F32)).astype(BF16))
            tail_all = (_dot(jnp.concatenate(his, axis=0), later)
                        + _dot(jnp.concatenate(los, axis=0), later))
            a_blocks = []
            for b, r in enumerate(range(0, tq, tb)):
                rows = slice(r, r + tb)
                run = run_ref[hh, rows, :]
                tail = tail_all[rows, :]
                a = jnp.exp2(log_betas[b] + tail + jnp.tile(run, (1, rep)))
                if masked:
                    a = jnp.where(start + col < row + r, a, 0.0)
                a_blocks.append(a.astype(BF16))
                run_new = run + row_sums[b]
                run_ref[hh, rows, :] = run_new
                run_top = run_new if run_top is None else jnp.maximum(run_top, run_new)
            acc_ref[hh] += _dot(jnp.concatenate(a_blocks, axis=0), v_ref[pl.ds(start, tk), sl])
        return jnp.max(run_top)

    run_max = None
    for d in range(nd):
        run_max = chunk((i + 1) * nd - 1 - d, True)

    def cond(carry):
        kb, run_max = carry
        return (kb >= 0) & (run_max > -UNDERFLOW * LOG2E)

    def body(carry):
        kb, _ = carry
        return kb - 1, chunk(kb, False)

    lax.while_loop(cond, body, (i * nd - 1, run_max))
    for hh in range(hp):
        sl = slice(hh * HEAD_DIM, (hh + 1) * HEAD_DIM)
        o_ref[:, sl] = acc_ref[hh].astype(BF16)


def _sb_attn(proj, *, tq, tk, hp, tb):
    S = proj.shape[0]
    w = hp * HEAD_DIM
    return pl.pallas_call(
        functools.partial(_sb_kernel, tq=tq, tk=tk, hp=hp, tb=tb),
        grid=(N_HEADS // hp, S // tq),
        in_specs=[
            pl.BlockSpec((tq, w), lambda g, i: (i, COL_QB // hp + g)),
            pl.BlockSpec((S, w), lambda g, i: (0, COL_KB // hp + g)),
            pl.BlockSpec((S, w), lambda g, i: (0, COL_VB // hp + g)),
        ],
        out_specs=pl.BlockSpec((tq, w), lambda g, i: (i, g)),
        out_shape=jax.ShapeDtypeStruct((S, W_ATT), BF16),
        scratch_shapes=[
            pltpu.VMEM((hp, tq, LANES), F32),
            pltpu.VMEM((hp, tq, HEAD_DIM), F32),
        ],
        compiler_params=_params("parallel", "arbitrary"),
        name="sb_attn",
    )(proj, proj, proj)


def _merge_kernel(ya_ref, yb_ref, wa_ref, wb_ref, ga_ref, gb_ref, *refs, side):
    side_in, (o_ref, *side_out) = refs[:len(side)], refs[len(side):]
    _side_cast(pl.program_id(0) * pl.num_programs(1) + pl.program_id(1), side, side_in, side_out)
    ya = ya_ref[...]
    yb = yb_ref[...]
    for sl in _col_blocks(o_ref.shape[1]):
        a = _dot(ya, wa_ref[:, sl])
        b = _dot(yb, wb_ref[:, sl])
        o_ref[:, sl] = (ga_ref[:, sl].astype(F32) * a + gb_ref[:, sl].astype(F32) * b).astype(BF16)


def _merge(y_a, y_b, w_a, w_b, proj, cast_weights, *, tm, tn):
    S, K = y_a.shape
    N = w_a.shape[1]
    ga0 = COL_GA * LANES // tn
    gb0 = COL_GB * LANES // tn
    nj = N // tn
    side_in, side_out, side_shapes, side = _side_cast_specs(
        cast_weights, (S // tm) * nj, lambda i, j: i * nj + j)
    return pl.pallas_call(
        functools.partial(_merge_kernel, side=side),
        grid=(S // tm, nj),
        in_specs=[
            pl.BlockSpec((tm, K), lambda i, j: (i, 0)),
            pl.BlockSpec((tm, K), lambda i, j: (i, 0)),
            pl.BlockSpec((K, tn), lambda i, j: (0, j)),
            pl.BlockSpec((K, tn), lambda i, j: (0, j)),
            pl.BlockSpec((tm, tn), lambda i, j: (i, ga0 + j)),
            pl.BlockSpec((tm, tn), lambda i, j: (i, gb0 + j)),
        ] + side_in,
        out_specs=[pl.BlockSpec((tm, tn), lambda i, j: (i, j))] + side_out,
        out_shape=[jax.ShapeDtypeStruct((S, N), BF16)] + side_shapes,
        compiler_params=_params("arbitrary", "arbitrary"),
        name="merge",
    )(y_a, y_b, w_a, w_b, proj, proj, *cast_weights)


def _matmul_res_kernel(a_ref, w_ref, x_ref, *refs, side):
    side_in, (o_ref, *side_out) = refs[:len(side)], refs[len(side):]
    _side_cast(pl.program_id(0) * pl.num_programs(1) + pl.program_id(1), side, side_in, side_out)
    a = a_ref[...]
    for sl in _col_blocks(o_ref.shape[1]):
        o_ref[:, sl] = x_ref[:, sl] + _dot(a, w_ref[:, sl])


def _matmul_res(a, w, x, cast_weights, *, tm, tn):
    S, K = a.shape
    N = w.shape[1]
    nj = N // tn
    side_in, side_out, side_shapes, side = _side_cast_specs(
        cast_weights, (S // tm) * nj, lambda i, j: i * nj + j)
    return pl.pallas_call(
        functools.partial(_matmul_res_kernel, side=side),
        grid=(S // tm, nj),
        in_specs=[
            pl.BlockSpec((tm, K), lambda i, j: (i, 0)),
            pl.BlockSpec((K, tn), lambda i, j: (0, j)),
            pl.BlockSpec((tm, tn), lambda i, j: (i, j)),
        ] + side_in,
        out_specs=[pl.BlockSpec((tm, tn), lambda i, j: (i, j))] + side_out,
        out_shape=[jax.ShapeDtypeStruct((S, N), F32)] + side_shapes,
        compiler_params=_params("arbitrary", "arbitrary"),
        name="out_res",
    )(a, w, x, *cast_weights)


def _up_glu_kernel(h_ref, wg32_ref, wv32_ref, cwg_ref, cwv_ref, cbg_ref, cbv_ref, o_ref,
                   wg0_ref, wg1_ref, wv0_ref, wv1_ref, ug0_ref, ug1_ref, uv0_ref, uv1_ref,
                   carry_g_ref, carry_v_ref, *, tm, ts, nj, n_tiles):
    t = pl.program_id(0)
    tile = jnp.clip(t - 1, 0, n_tiles - 1)
    j = tile % nj
    wg_refs, wv_refs = (wg0_ref, wg1_ref), (wv0_ref, wv1_ref)
    ug_refs, uv_refs = (ug0_ref, ug1_ref), (uv0_ref, uv1_ref)
    tc = min(ts, 4 * SUBLANES)
    kc = h_ref.shape[1] * tc // ts
    halo = SUBLANES
    first = tile // nj == 0

    @pl.when(t == 0)
    def _():
        wg1_ref[...] = jnp.zeros_like(wg1_ref)
        wv1_ref[...] = jnp.zeros_like(wv1_ref)
        ug0_ref[...] = jnp.zeros_like(ug0_ref)
        uv0_ref[...] = jnp.zeros_like(uv0_ref)

    def cast_chunk(slot, ks):
        wg_refs[slot][ks, :] = wg32_ref[ks, :].astype(BF16)
        wv_refs[slot][ks, :] = wv32_ref[ks, :].astype(BF16)

    def epilogue_chunk(slot, r):
        def conv(u_ref, cw_ref, cb_ref):
            cw = cw_ref[...]
            rows = lambda back: u_ref[halo - back + r:halo - back + r + tc, :]
            return cb_ref[...] + cw[0:1, :] * rows(2) + cw[1:2, :] * rows(1) + cw[2:3, :] * rows(0)

        gate = conv(ug_refs[slot], cwg_ref, cbg_ref)
        val = conv(uv_refs[slot], cwv_ref, cbv_ref)
        o_ref[r:r + tc, :] = (gate * jax.nn.sigmoid(gate) * val).astype(BF16)

    def rows_step(p, r0):
        acc_g = acc_v = None
        for c in range(ts // tc):
            ks = slice(c * kc, (c + 1) * kc)
            lhs = h_ref[r0:r0 + ts, ks]
            dg = _dot(lhs, wg_refs[1 - p][ks, :])
            dv = _dot(lhs, wv_refs[1 - p][ks, :])
            acc_g = dg if acc_g is None else acc_g + dg
            acc_v = dv if acc_v is None else acc_v + dv
            if r0 == 0:
                cast_chunk(p, ks)
            epilogue_chunk(p, r0 + c * tc)
        for u, u_ref, carry_ref in ((acc_g, ug_refs[1 - p], carry_g_ref),
                                    (acc_v, uv_refs[1 - p], carry_v_ref)):
            if r0 == 0:
                u_ref[0:halo, :] = jnp.where(first, 0.0, carry_ref[j])
            u_ref[halo + r0:halo + r0 + ts, :] = u
            if r0 + ts == tm:
                carry_ref[j] = u[ts - halo:ts, :]

    for parity in (0, 1):
        @pl.when(t % 2 == parity)
        def _():
            for r0 in range(0, tm, ts):
                rows_step(parity, r0)


def _up_glu(h, w_up, conv_w, conv_b, *, tm, tn, ts):
    S, D = h.shape
    d_ff = w_up.shape[1] // 2
    nj = d_ff // tn
    n_tiles = (S // tm) * nj
    cast = lambda t: jnp.minimum(t, n_tiles - 1)
    mult = lambda t: jnp.clip(t - 1, 0, n_tiles - 1)
    done = lambda t: jnp.clip(t - 2, 0, n_tiles - 1)
    wbuf = pltpu.VMEM((D, tn), BF16)
    ubuf = pltpu.VMEM((tm + SUBLANES, tn), F32)
    return pl.pallas_call(
        functools.partial(_up_glu_kernel, tm=tm, ts=ts, nj=nj, n_tiles=n_tiles),
        grid=(n_tiles + 2,),
        in_specs=[
            pl.BlockSpec((tm, D), lambda t: (mult(t) // nj, 0), pipeline_mode=pl.Buffered(1)),
            pl.BlockSpec((D, tn), lambda t: (0, cast(t) % nj)),
            pl.BlockSpec((D, tn), lambda t: (0, nj + cast(t) % nj)),
            pl.BlockSpec((CONV_WIDTH, tn), lambda t: (0, done(t) % nj)),
            pl.BlockSpec((CONV_WIDTH, tn), lambda t: (0, nj + done(t) % nj)),
            pl.BlockSpec((1, tn), lambda t: (0, done(t) % nj)),
            pl.BlockSpec((1, tn), lambda t: (0, nj + done(t) % nj)),
        ],
        out_specs=pl.BlockSpec((tm, tn), lambda t: (done(t) // nj, done(t) % nj)),
        out_shape=jax.ShapeDtypeStruct((S, d_ff), BF16),
        scratch_shapes=[wbuf, wbuf, wbuf, wbuf, ubuf, ubuf, ubuf, ubuf,
                        pltpu.VMEM((nj, SUBLANES, tn), F32),
                        pltpu.VMEM((nj, SUBLANES, tn), F32)],
        compiler_params=_params("arbitrary"),
        name="up_glu",
    )(h, w_up, w_up, conv_w, conv_w, conv_b, conv_b)


def _down_res_kernel(a_ref, w_ref, x_ref, o_ref, acc_ref):
    k = pl.program_id(2)

    @pl.when(k == 0)
    def _():
        acc_ref[...] = x_ref[...]

    acc_ref[...] += _dot(a_ref[...], w_ref[...])

    @pl.when(k == pl.num_programs(2) - 1)
    def _():
        o_ref[...] = acc_ref[...]


def _down_res(a, w, x, *, tm, tn, tk):
    S, K = a.shape
    N = w.shape[1]
    return pl.pallas_call(
        _down_res_kernel,
        grid=(S // tm, N // tn, K // tk),
        in_specs=[
            pl.BlockSpec((tm, tk), lambda i, j, k: (i, k)),
            pl.BlockSpec((tk, tn), lambda i, j, k: (k, j)),
            pl.BlockSpec((tm, tn), lambda i, j, k: (i, j)),
        ],
        out_specs=pl.BlockSpec((tm, tn), lambda i, j, k: (i, j)),
        out_shape=jax.ShapeDtypeStruct((S, N), F32),
        scratch_shapes=[pltpu.VMEM((tm, tn), F32)],
        compiler_params=_params("parallel", "parallel", "arbitrary"),
        name="down_res",
    )(a, w, x)


def _ple_kernel(h_ref, wg_ref, p_ref, wp_ref, x_ref, o_ref):
    h = h_ref[...]
    p = p_ref[...].astype(BF16)
    for sl in _col_blocks(o_ref.shape[1]):
        gate = jax.nn.sigmoid(_dot(h, wg_ref[:, sl]))
        o_ref[:, sl] = x_ref[:, sl] + gate * _dot(p, wp_ref[:, sl])


def _ple(h, w_gate, p, w_proj, x, *, tm, tn):
    S, D = h.shape
    P = p.shape[1]
    return pl.pallas_call(
        _ple_kernel,
        grid=(S // tm, D // tn),
        in_specs=[
            pl.BlockSpec((tm, D), lambda i, j: (i, 0)),
            pl.BlockSpec((D, tn), lambda i, j: (0, j)),
            pl.BlockSpec((tm, P), lambda i, j: (i, 0)),
            pl.BlockSpec((P, tn), lambda i, j: (0, j)),
            pl.BlockSpec((tm, tn), lambda i, j: (i, j)),
        ],
        out_specs=pl.BlockSpec((tm, tn), lambda i, j: (i, j)),
        out_shape=jax.ShapeDtypeStruct((S, D), F32),
        compiler_params=_params("parallel", "arbitrary"),
        name="ple",
    )(h, w_gate, p, w_proj, x)


def _tiles(S):
    return dict(
        norm=dict(tr=min(256, S)),
        proj=dict(tm=min(2048, S), ts=min(512, S), tn=512),
        cumsum=dict(tb=min(512, S)),
        fox=dict(tq=min(512, S), tk=min(512, S), hp=2, tb=32),
        sb=dict(tq=min(256, S), tk=min(256, S), hp=4, tb=32),
        merge=dict(tm=min(1024, S), tn=512),
        out=dict(tm=min(1024, S), tn=512),
        up=dict(tm=min(2048, S), tn=256, ts=min(512, S)),
        down=dict(tm=min(1024, S), tn=512, tk=5504),
        ple=dict(tm=min(1024, S), tn=512),
    )


def _layer(x, p, g_mix, w_in, b_f, g_q, g_k, w_bf, w_bs, w_out, g_ffn, w_up, conv_w, conv_b,
           w_down, g_ple, w_ple_gate, w_ple_proj, tiles):
    S, D = x.shape
    f0 = 3 * W_ATT
    sb0 = f0 + N_HEADS
    w_t = w_in.T
    w_f = w_t[f0:f0 + LANES].astype(BF16)
    b_f_row = jnp.pad(b_f[None, :], ((0, 0), (0, LANES - N_HEADS)))

    h, log_f = _norm(x, g_mix[None, :], w_f=w_f, b_f=b_f_row, **tiles["norm"])
    per = W_ATT // tiles["proj"]["tn"]
    proj_a = _proj("proj_fox", h, w_t, 0, ((per, "q_norm"), (per, "k_norm"), (per, "plain")),
                   (g_q[None, :], g_k[None, :]), **tiles["proj"])
    gate_tiles = (w_t.shape[0] - sb0) // tiles["proj"]["tn"] - 3 * per
    proj_b = _proj("proj_sb", h, w_t, sb0,
                   ((per, "q_scale"), (2 * per, "plain"), (gate_tiles, "sigmoid")), (),
                   **tiles["proj"])
    c = _cumsum(log_f, **tiles["cumsum"])
    tk = tiles["fox"]["tk"]
    c_rows = c[:, :N_HEADS].T.reshape(N_HEADS, S // tk, 1, tk)
    c_edges = c_rows[:, :, 0, tk - 1]
    qk_max = HEAD_DIM ** 0.5 * LOG2E * QK_SLACK * jnp.max(jnp.abs(g_q)) * jnp.max(jnp.abs(g_k))
    lim = ((UNDERFLOW + C_SLACK) * LOG2E + 2.0 * qk_max).reshape(1)
    y_a, w_bf16, w_bs16, w_out16 = _fox_attn(proj_a, c_rows, c_edges, lim, [w_bf, w_bs, w_out],
                                             **tiles["fox"])
    y_b = _sb_attn(proj_b, **tiles["sb"])
    merged, w_down16 = _merge(y_a, y_b, w_bf16, w_bs16, proj_b, [w_down], **tiles["merge"])
    x, w_gate16 = _matmul_res(merged, w_out16, x, [w_ple_gate], **tiles["out"])
    h = _norm(x, g_ffn[None, :], **tiles["norm"])
    gated = _up_glu(h, w_up, conv_w, conv_b[None, :], **tiles["up"])
    x = _down_res(gated, w_down16, x, **tiles["down"])
    h = _norm(x, g_ple[None, :], **tiles["norm"])
    x = _ple(h, w_gate16, p, w_ple_proj.astype(BF16), x, **tiles["ple"])
    return x


def _forward(x, p, g_mix, w_in, b_f, g_q_fox, g_k_fox, w_branch_fox, w_branch_sb, w_out, g_ffn,
             w_up, conv_w, conv_b, w_down, g_ple, w_ple_gate, w_ple_proj, tiles):
    B, S, D = x.shape
    outs = []
    for b in range(B):
        xb = x[b]
        for i in range(w_in.shape[0]):
            xb = _layer(xb, p[i, b], g_mix[i], w_in[i], b_f[i], g_q_fox[i], g_k_fox[i],
                        w_branch_fox[i], w_branch_sb[i], w_out[i], g_ffn[i], w_up[i], conv_w[i],
                        conv_b[i], w_down[i], g_ple[i], w_ple_gate[i], w_ple_proj[i], tiles)
        outs.append(xb)
    return jnp.stack(outs, axis=0)


def kernel(x, p, g_mix, w_in, b_f, g_q_fox, g_k_fox, w_branch_fox, w_branch_sb, w_out, g_ffn,
           w_up, conv_w, conv_b, w_down, g_ple, w_ple_gate, w_ple_proj):
    return _forward(x, p, g_mix, w_in, b_f, g_q_fox, g_k_fox, w_branch_fox, w_branch_sb, w_out,
                    g_ffn, w_up, conv_w, conv_b, w_down, g_ple, w_ple_gate, w_ple_proj,
                    _tiles(x.shape[1]))
```

```python
import functools

import jax
import jax.numpy as jnp
from jax import lax
from jax.experimental import pallas as pl
from jax.experimental.pallas import tpu as pltpu

F32 = jnp.float32
BF16 = jnp.bfloat16

EPS = 1e-6
HEAD_DIM = 128
N_HEADS = 16
W_ATT = N_HEADS * HEAD_DIM
LANES = 128
SUBLANES = 8
MXU_COLS = 256
CONV_WIDTH = 3
VMEM_LIMIT_BYTES = 56 * 1024 * 1024
UNDERFLOW = 105.0
LOG2E = 1.4426950408889634
C_SLACK = 2.0
QK_SLACK = 1.02

COL_QA, COL_KA, COL_VA = 0, 16, 32
COL_QB, COL_KB, COL_VB, COL_GA, COL_GB = 0, 16, 32, 48, 80


def _params(*sem):
    return pltpu.CompilerParams(dimension_semantics=sem, vmem_limit_bytes=VMEM_LIMIT_BYTES)


def _log_sigmoid(z):
    return jnp.minimum(z, 0.0) - jnp.log1p(jnp.exp(-jnp.abs(z)))


def _rms_norm_rows(x, g):
    ms = jnp.mean(x * x, axis=-1, keepdims=True)
    return x * lax.rsqrt(ms + EPS) * g


def _dot(a, b):
    return jnp.dot(a, b, preferred_element_type=F32)


def _dot_nt(a, b):
    return lax.dot_general(a, b, (((1,), (1,)), ((), ())), preferred_element_type=F32)


def _col_blocks(width):
    return [slice(s, s + MXU_COLS) for s in range(0, width, MXU_COLS)]


def _side_cast_specs(weights, n_steps, step_of):
    rows = LANES
    while sum(w.shape[0] // rows for w in weights) > n_steps:
        rows *= 2
    in_specs, out_specs, out_shapes, plan = [], [], [], []
    first = 0
    for w in weights:
        assert w.shape[0] % rows == 0, (w.shape, rows)
        nb = w.shape[0] // rows

        def index(*ids, first=first, nb=nb):
            return jnp.clip(step_of(*ids) - first, 0, nb - 1), 0

        in_specs.append(pl.BlockSpec((rows, w.shape[1]), index))
        out_specs.append(pl.BlockSpec((rows, w.shape[1]), index))
        out_shapes.append(jax.ShapeDtypeStruct(w.shape, BF16))
        plan.append((first, nb))
        first += nb
    return in_specs, out_specs, out_shapes, tuple(plan)


def _side_cast(step, plan, in_refs, out_refs):
    for (first, nb), i_ref, o_ref in zip(plan, in_refs, out_refs):
        @pl.when((step >= first) & (step < first + nb))
        def _():
            o_ref[...] = i_ref[...].astype(BF16)


def _norm_kernel(x_ref, g_ref, o_ref):
    o_ref[...] = _rms_norm_rows(x_ref[...], g_ref[...]).astype(BF16)


def _norm_gate_kernel(x_ref, g_ref, wf_ref, bf_ref, o_ref, lf_ref):
    h = _rms_norm_rows(x_ref[...], g_ref[...]).astype(BF16)
    o_ref[...] = h
    lf_ref[...] = _log_sigmoid(_dot_nt(h, wf_ref[...]) + bf_ref[...]) * LOG2E


def _norm(x, g, *, tr, w_f=None, b_f=None):
    S, D = x.shape
    row = pl.BlockSpec((tr, D), lambda i: (i, 0))
    gain = pl.BlockSpec((1, D), lambda i: (0, 0))
    if w_f is None:
        return pl.pallas_call(
            _norm_kernel, grid=(S // tr,), in_specs=[row, gain], out_specs=row,
            out_shape=jax.ShapeDtypeStruct((S, D), BF16),
            compiler_params=_params("parallel"), name="norm",
        )(x, g)
    return pl.pallas_call(
        _norm_gate_kernel, grid=(S // tr,),
        in_specs=[row, gain, pl.BlockSpec((LANES, D), lambda i: (0, 0)),
                  pl.BlockSpec((1, LANES), lambda i: (0, 0))],
        out_specs=[row, pl.BlockSpec((tr, LANES), lambda i: (i, 0))],
        out_shape=[jax.ShapeDtypeStruct((S, D), BF16), jax.ShapeDtypeStruct((S, LANES), F32)],
        compiler_params=_params("parallel"), name="norm_gate",
    )(x, g, w_f, b_f)


def _proj_step(h_ref, w32_ref, o_ref, wb_cast_ref, wb_mult_ref, epilogue, *, tm, ts):
    tn, d = w32_ref.shape
    kc = MXU_COLS
    for r0 in range(0, tm, ts):
        accs = [None] * (tn // MXU_COLS)
        for c in range(d // kc):
            ks = slice(c * kc, (c + 1) * kc)
            lhs = h_ref[r0:r0 + ts, ks]
            for s, sl in enumerate(_col_blocks(tn)):
                part = _dot(lhs, wb_mult_ref[ks, sl])
                accs[s] = part if accs[s] is None else accs[s] + part
            if r0 == 0:
                wb_cast_ref[ks, :] = w32_ref[:, ks].T.astype(BF16)
        for s, sl in enumerate(_col_blocks(tn)):
            o_ref[r0:r0 + ts, sl] = epilogue(accs[s]).astype(BF16)


def _proj_kernel(h_ref, w32_ref, *refs, tm, ts, nj, n_tiles, sections):
    *gain_refs, o_ref, wb0_ref, wb1_ref = refs
    wb_refs = (wb0_ref, wb1_ref)
    t = pl.program_id(0)
    j = jnp.clip(t - 1, 0, n_tiles - 1) % nj

    @pl.when(t == 0)
    def _():
        wb1_ref[...] = jnp.zeros_like(wb1_ref)

    def head_norm(g_ref, mult):
        def epilogue(acc):
            heads = [_rms_norm_rows(acc[:, c:c + HEAD_DIM], g_ref[...]) * mult
                     for c in range(0, acc.shape[1], HEAD_DIM)]
            return jnp.concatenate(heads, axis=1)
        return epilogue

    scale = HEAD_DIM ** -0.5 * LOG2E
    epilogues = {
        "q_norm": lambda: head_norm(gain_refs[0], scale),
        "k_norm": lambda: head_norm(gain_refs[1], 1.0),
        "q_scale": lambda: (lambda acc: acc * scale),
        "plain": lambda: (lambda acc: acc),
        "sigmoid": lambda: jax.nn.sigmoid,
    }
    for lo, hi, kind in sections:
        for parity in (0, 1):
            @pl.when((j >= lo) & (j < hi) & (t % 2 == parity))
            def _():
                _proj_step(h_ref, w32_ref, o_ref, wb_refs[parity], wb_refs[1 - parity],
                           epilogues[kind](), tm=tm, ts=ts)


def _proj(name, h, w_t, row0, sections, gains, *, tm, ts, tn):
    S, D = h.shape
    nj = sum(n for n, _ in sections)
    bounds, lo = [], 0
    for n, kind in sections:
        bounds.append((lo, lo + n, kind))
        lo += n
    n_tiles = (S // tm) * nj
    cast = lambda t: jnp.minimum(t, n_tiles - 1)
    mult = lambda t: jnp.clip(t - 1, 0, n_tiles - 1)
    return pl.pallas_call(
        functools.partial(_proj_kernel, tm=tm, ts=ts, nj=nj, n_tiles=n_tiles,
                          sections=tuple(bounds)),
        grid=(n_tiles + 1,),
        in_specs=[pl.BlockSpec((tm, D), lambda t: (mult(t) // nj, 0), pipeline_mode=pl.Buffered(1)),
                  pl.BlockSpec((pl.Element(tn), pl.Element(D)),
                               lambda t: (pl.multiple_of(row0 + (cast(t) % nj) * tn, N_HEADS), 0))]
                 + [pl.BlockSpec((1, HEAD_DIM), lambda t: (0, 0)) for _ in gains],
        out_specs=pl.BlockSpec((tm, tn), lambda t: (mult(t) // nj, mult(t) % nj)),
        out_shape=jax.ShapeDtypeStruct((S, nj * tn), BF16),
        scratch_shapes=[pltpu.VMEM((D, tn), BF16), pltpu.VMEM((D, tn), BF16)],
        compiler_params=_params("arbitrary"),
        name=name,
    )(h, w_t, *gains)


def _cumsum_kernel(x_ref, o_ref, carry_ref, *, tb):
    @pl.when(pl.program_id(0) == 0)
    def _():
        carry_ref[...] = jnp.zeros_like(carry_ref)

    x = x_ref[...]
    r = lax.broadcasted_iota(jnp.int32, (tb, tb), 0)
    c = lax.broadcasted_iota(jnp.int32, (tb, tb), 1)
    tri = (c <= r).astype(BF16)
    x1 = x.astype(BF16)
    r1 = x - x1.astype(F32)
    x2 = r1.astype(BF16)
    x3 = (r1 - x2.astype(F32)).astype(BF16)
    cs = _dot(tri, x1) + _dot(tri, x2) + _dot(tri, x3) + carry_ref[...]
    o_ref[...] = cs
    carry_ref[...] = cs[tb - 1:tb, :]


def _cumsum(x, *, tb):
    S, W = x.shape
    return pl.pallas_call(
        functools.partial(_cumsum_kernel, tb=tb),
        grid=(S // tb,),
        in_specs=[pl.BlockSpec((tb, W), lambda i: (i, 0))],
        out_specs=pl.BlockSpec((tb, W), lambda i: (i, 0)),
        out_shape=jax.ShapeDtypeStruct((S, W), F32),
        scratch_shapes=[pltpu.VMEM((1, W), F32)],
        compiler_params=_params("arbitrary"),
        name="cumsum_logf",
    )(x)


def _fox_kernel(edge_ref, lim_ref, q_ref, k_ref, v_ref, c_ref, *refs, tq, tk, hp, tb, side):
    side_in, (o_ref, *side_out), (m_ref, l_ref, acc_ref) = (
        refs[:len(side)], refs[len(side):2 * len(side) + 1], refs[2 * len(side) + 1:])
    g = pl.program_id(0)
    i = pl.program_id(1)
    _side_cast(g * pl.num_programs(1) + i, side, side_in, side_out)
    nd = tq // tk
    rep = tk // LANES
    m_ref[...] = jnp.full_like(m_ref, -1e30)
    l_ref[...] = jnp.zeros_like(l_ref)
    acc_ref[...] = jnp.zeros_like(acc_ref)
    row = i * tq + lax.broadcasted_iota(jnp.int32, (tb, tk), 0)
    col = lax.broadcasted_iota(jnp.int32, (tb, tk), 1)

    def step(kb, masked):
        start = pl.multiple_of(kb * tk, tk)
        for hh in range(hp):
            sl = slice(hh * HEAD_DIM, (hh + 1) * HEAD_DIM)
            s_all = _dot_nt(q_ref[:, sl], k_ref[pl.ds(start, tk), sl])
            c_row = c_ref[hh, kb]
            p_blocks = []
            for r in range(0, tq, tb):
                rows = slice(r, r + tb)
                s = s_all[rows, :] - c_row
                if masked:
                    s = jnp.where(start + col <= row + r, s, -jnp.inf)
                m_old = m_ref[hh, rows, :]
                m_new = jnp.maximum(m_old, jnp.max(s, axis=-1, keepdims=True))
                alpha = jnp.exp2(m_old - m_new)
                p = jnp.exp2(s - jnp.tile(m_new, (1, rep)))
                l_ref[hh, rows, :] = alpha * l_ref[hh, rows, :] + jnp.sum(p, axis=-1, keepdims=True)
                acc_ref[hh, rows, :] = alpha * acc_ref[hh, rows, :]
                m_ref[hh, rows, :] = m_new
                p_blocks.append(p.astype(BF16))
            p_all = jnp.concatenate(p_blocks, axis=0)
            acc_ref[hh] += _dot(p_all, v_ref[pl.ds(start, tk), sl])

    for d in range(nd):
        step((i + 1) * nd - 1 - d, True)

    top = i * nd - 1
    lim = lim_ref[0]

    def chunks_needed(h):
        e_hi = edge_ref[h, jnp.maximum(top, 0)]

        def needed(kb):
            return (kb >= 0) & (e_hi - edge_ref[h, jnp.maximum(kb, 0)] > -lim)

        return top - lax.while_loop(needed, lambda kb: kb - 1, top)

    n = chunks_needed(g * hp)
    for hh in range(1, hp):
        n = jnp.maximum(n, chunks_needed(g * hp + hh))

    def body(t, carry):
        step(top - t, False)
        return carry

    lax.fori_loop(0, n, body, 0)
    for hh in range(hp):
        sl = slice(hh * HEAD_DIM, (hh + 1) * HEAD_DIM)
        o_ref[:, sl] = (acc_ref[hh] / l_ref[hh]).astype(BF16)


def _fox_attn(proj, c_rows, c_edges, lim, cast_weights, *, tq, tk, hp, tb):
    S = proj.shape[0]
    nkb = S // tk
    w = hp * HEAD_DIM
    ni = S // tq
    side_in, side_out, side_shapes, side = _side_cast_specs(
        cast_weights, (N_HEADS // hp) * ni, lambda g, i: g * ni + i)
    return pl.pallas_call(
        functools.partial(_fox_kernel, tq=tq, tk=tk, hp=hp, tb=tb, side=side),
        grid=(N_HEADS // hp, ni),
        in_specs=[
            pl.BlockSpec(memory_space=pltpu.SMEM),
            pl.BlockSpec(memory_space=pltpu.SMEM),
            pl.BlockSpec((tq, w), lambda g, i: (i, COL_QA // hp + g)),
            pl.BlockSpec((S, w), lambda g, i: (0, COL_KA // hp + g)),
            pl.BlockSpec((S, w), lambda g, i: (0, COL_VA // hp + g)),
            pl.BlockSpec((hp, nkb, 1, tk), lambda g, i: (g, 0, 0, 0)),
        ] + side_in,
        out_specs=[pl.BlockSpec((tq, w), lambda g, i: (i, g))] + side_out,
        out_shape=[jax.ShapeDtypeStruct((S, W_ATT), BF16)] + side_shapes,
        scratch_shapes=[
            pltpu.VMEM((hp, tq, LANES), F32),
            pltpu.VMEM((hp, tq, LANES), F32),
            pltpu.VMEM((hp, tq, HEAD_DIM), F32),
        ],
        compiler_params=_params("arbitrary", "arbitrary"),
        name="fox_attn",
    )(c_edges, lim, proj, proj, proj, c_rows, *cast_weights)


def _sb_kernel(q_ref, k_ref, v_ref, o_ref, run_ref, acc_ref, *, tq, tk, hp, tb):
    i = pl.program_id(1)
    nd = tq // tk
    rep = tk // LANES
    run_ref[...] = jnp.zeros_like(run_ref)
    acc_ref[...] = jnp.zeros_like(acc_ref)
    row = i * tq + lax.broadcasted_iota(jnp.int32, (tb, tk), 0)
    col = lax.broadcasted_iota(jnp.int32, (tb, tk), 1)
    jj = lax.broadcasted_iota(jnp.int32, (tk, tk), 0)
    ss = lax.broadcasted_iota(jnp.int32, (tk, tk), 1)
    later = (jj > ss).astype(BF16)

    def chunk(kb, masked):
        start = pl.multiple_of(kb * tk, tk)
        run_top = None
        for hh in range(hp):
            sl = slice(hh * HEAD_DIM, (hh + 1) * HEAD_DIM)
            z_all = _dot_nt(q_ref[:, sl], k_ref[pl.ds(start, tk), sl])
            log_betas, his, los, row_sums = [], [], [], []
            for r in range(0, tq, tb):
                z = z_all[r:r + tb, :]
                log_beta = jnp.minimum(z, 0.0) - jnp.log2(1.0 + jnp.exp2(-jnp.abs(z)))
                l1m = log_beta - z
                if masked:
                    l1m = jnp.where(start + col < row + r, l1m, 0.0)
                hi = l1m.astype(BF16)
                row_sums.append(jnp.sum(l1m, axis=-1, keepdims=True))
                log_betas.append(log_beta)
                his.append(hi)
                los.append((l1m - hi.astype(F32)).astype(BF16))
            tail_all = (_dot(jnp.concatenate(his, axis=0), later)
                        + _dot(jnp.concatenate(los, axis=0), later))
            a_blocks = []
            for b, r in enumerate(range(0, tq, tb)):
                rows = slice(r, r + tb)
                run = run_ref[hh, rows, :]
                tail = tail_all[rows, :]
                a = jnp.exp2(log_betas[b] + tail + jnp.tile(run, (1, rep)))
                if masked:
                    a = jnp.where(start + col < row + r, a, 0.0)
                a_blocks.append(a.astype(BF16))
                run_new = run + row_sums[b]
                run_ref[hh, rows, :] = run_new
                run_top = run_new if run_top is None else jnp.maximum(run_top, run_new)
            acc_ref[hh] += _dot(jnp.concatenate(a_blocks, axis=0), v_ref[pl.ds(start, tk), sl])
        return jnp.max(run_top)

    run_max = None
    for d in range(nd):
        run_max = chunk((i + 1) * nd - 1 - d, True)

    def cond(carry):
        kb, run_max = carry
        return (kb >= 0) & (run_max > -UNDERFLOW * LOG2E)

    def body(carry):
        kb, _ = carry
        return kb - 1, chunk(kb, False)

    lax.while_loop(cond, body, (i * nd - 1, run_max))
    for hh in range(hp):
        sl = slice(hh * HEAD_DIM, (hh + 1) * HEAD_DIM)
        o_ref[:, sl] = acc_ref[hh].astype(BF16)


def _sb_attn(proj, *, tq, tk, hp, tb):
    S = proj.shape[0]
    w = hp * HEAD_DIM
    return pl.pallas_call(
        functools.partial(_sb_kernel, tq=tq, tk=tk, hp=hp, tb=tb),
        grid=(N_HEADS // hp, S // tq),
        in_specs=[
            pl.BlockSpec((tq, w), lambda g, i: (i, COL_QB // hp + g)),
            pl.BlockSpec((S, w), lambda g, i: (0, COL_KB // hp + g)),
            pl.BlockSpec((S, w), lambda g, i: (0, COL_VB // hp + g)),
        ],
        out_specs=pl.BlockSpec((tq, w), lambda g, i: (i, g)),
        out_shape=jax.ShapeDtypeStruct((S, W_ATT), BF16),
        scratch_shapes=[
            pltpu.VMEM((hp, tq, LANES), F32),
            pltpu.VMEM((hp, tq, HEAD_DIM), F32),
        ],
        compiler_params=_params("parallel", "arbitrary"),
        name="sb_attn",
    )(proj, proj, proj)


def _merge_kernel(ya_ref, yb_ref, wa_ref, wb_ref, ga_ref, gb_ref, *refs, side):
    side_in, (o_ref, *side_out) = refs[:len(side)], refs[len(side):]
    _side_cast(pl.program_id(0) * pl.num_programs(1) + pl.program_id(1), side, side_in, side_out)
    ya = ya_ref[...]
    yb = yb_ref[...]
    for sl in _col_blocks(o_ref.shape[1]):
        a = _dot(ya, wa_ref[:, sl])
        b = _dot(yb, wb_ref[:, sl])
        o_ref[:, sl] = (ga_ref[:, sl].astype(F32) * a + gb_ref[:, sl].astype(F32) * b).astype(BF16)


def _merge(y_a, y_b, w_a, w_b, proj, cast_weights, *, tm, tn):
    S, K = y_a.shape
    N = w_a.shape[1]
    ga0 = COL_GA * LANES // tn
    gb0 = COL_GB * LANES // tn
    nj = N // tn
    side_in, side_out, side_shapes, side = _side_cast_specs(
        cast_weights, (S // tm) * nj, lambda i, j: i * nj + j)
    return pl.pallas_call(
        functools.partial(_merge_kernel, side=side),
        grid=(S // tm, nj),
        in_specs=[
            pl.BlockSpec((tm, K), lambda i, j: (i, 0)),
            pl.BlockSpec((tm, K), lambda i, j: (i, 0)),
            pl.BlockSpec((K, tn), lambda i, j: (0, j)),
            pl.BlockSpec((K, tn), lambda i, j: (0, j)),
            pl.BlockSpec((tm, tn), lambda i, j: (i, ga0 + j)),
            pl.BlockSpec((tm, tn), lambda i, j: (i, gb0 + j)),
        ] + side_in,
        out_specs=[pl.BlockSpec((tm, tn), lambda i, j: (i, j))] + side_out,
        out_shape=[jax.ShapeDtypeStruct((S, N), BF16)] + side_shapes,
        compiler_params=_params("arbitrary", "arbitrary"),
        name="merge",
    )(y_a, y_b, w_a, w_b, proj, proj, *cast_weights)


def _matmul_res_kernel(a_ref, w_ref, x_ref, *refs, side):
    side_in, (o_ref, *side_out) = refs[:len(side)], refs[len(side):]
    _side_cast(pl.program_id(0) * pl.num_programs(1) + pl.program_id(1), side, side_in, side_out)
    a = a_ref[...]
    for sl in _col_blocks(o_ref.shape[1]):
        o_ref[:, sl] = x_ref[:, sl] + _dot(a, w_ref[:, sl])


def _matmul_res(a, w, x, cast_weights, *, tm, tn):
    S, K = a.shape
    N = w.shape[1]
    nj = N // tn
    side_in, side_out, side_shapes, side = _side_cast_specs(
        cast_weights, (S // tm) * nj, lambda i, j: i * nj + j)
    return pl.pallas_call(
        functools.partial(_matmul_res_kernel, side=side),
        grid=(S // tm, nj),
        in_specs=[
            pl.BlockSpec((tm, K), lambda i, j: (i, 0)),
            pl.BlockSpec((K, tn), lambda i, j: (0, j)),
            pl.BlockSpec((tm, tn), lambda i, j: (i, j)),
        ] + side_in,
        out_specs=[pl.BlockSpec((tm, tn), lambda i, j: (i, j))] + side_out,
        out_shape=[jax.ShapeDtypeStruct((S, N), F32)] + side_shapes,
        compiler_params=_params("arbitrary", "arbitrary"),
        name="out_res",
    )(a, w, x, *cast_weights)


def _up_glu_kernel(h_ref, wg32_ref, wv32_ref, cwg_ref, cwv_ref, cbg_ref, cbv_ref, o_ref,
                   wg0_ref, wg1_ref, wv0_ref, wv1_ref, ug0_ref, ug1_ref, uv0_ref, uv1_ref,
                   carry_g_ref, carry_v_ref, *, tm, ts, nj, n_tiles):
    t = pl.program_id(0)
    tile = jnp.clip(t - 1, 0, n_tiles - 1)
    j = tile % nj
    wg_refs, wv_refs = (wg0_ref, wg1_ref), (wv0_ref, wv1_ref)
    ug_refs, uv_refs = (ug0_ref, ug1_ref), (uv0_ref, uv1_ref)
    tc = min(ts, 4 * SUBLANES)
    kc = h_ref.shape[1] * tc // ts
    halo = SUBLANES
    first = tile // nj == 0

    @pl.when(t == 0)
    def _():
        wg1_ref[...] = jnp.zeros_like(wg1_ref)
        wv1_ref[...] = jnp.zeros_like(wv1_ref)
        ug0_ref[...] = jnp.zeros_like(ug0_ref)
        uv0_ref[...] = jnp.zeros_like(uv0_ref)

    def cast_chunk(slot, ks):
        wg_refs[slot][ks, :] = wg32_ref[ks, :].astype(BF16)
        wv_refs[slot][ks, :] = wv32_ref[ks, :].astype(BF16)

    def epilogue_chunk(slot, r):
        def conv(u_ref, cw_ref, cb_ref):
            cw = cw_ref[...]
            rows = lambda back: u_ref[halo - back + r:halo - back + r + tc, :]
            return cb_ref[...] + cw[0:1, :] * rows(2) + cw[1:2, :] * rows(1) + cw[2:3, :] * rows(0)

        gate = conv(ug_refs[slot], cwg_ref, cbg_ref)
        val = conv(uv_refs[slot], cwv_ref, cbv_ref)
        o_ref[r:r + tc, :] = (gate * jax.nn.sigmoid(gate) * val).astype(BF16)

    def rows_step(p, r0):
        acc_g = acc_v = None
        for c in range(ts // tc):
            ks = slice(c * kc, (c + 1) * kc)
            lhs = h_ref[r0:r0 + ts, ks]
            dg = _dot(lhs, wg_refs[1 - p][ks, :])
            dv = _dot(lhs, wv_refs[1 - p][ks, :])
            acc_g = dg if acc_g is None else acc_g + dg
            acc_v = dv if acc_v is None else acc_v + dv
            if r0 == 0:
                cast_chunk(p, ks)
            epilogue_chunk(p, r0 + c * tc)
        for u, u_ref, carry_ref in ((acc_g, ug_refs[1 - p], carry_g_ref),
                                    (acc_v, uv_refs[1 - p], carry_v_ref)):
            if r0 == 0:
                u_ref[0:halo, :] = jnp.where(first, 0.0, carry_ref[j])
            u_ref[halo + r0:halo + r0 + ts, :] = u
            if r0 + ts == tm:
                carry_ref[j] = u[ts - halo:ts, :]

    for parity in (0, 1):
        @pl.when(t % 2 == parity)
        def _():
            for r0 in range(0, tm, ts):
                rows_step(parity, r0)


def _up_glu(h, w_up, conv_w, conv_b, *, tm, tn, ts):
    S, D = h.shape
    d_ff = w_up.shape[1] // 2
    nj = d_ff // tn
    n_tiles = (S // tm) * nj
    cast = lambda t: jnp.minimum(t, n_tiles - 1)
    mult = lambda t: jnp.clip(t - 1, 0, n_tiles - 1)
    done = lambda t: jnp.clip(t - 2, 0, n_tiles - 1)
    wbuf = pltpu.VMEM((D, tn), BF16)
    ubuf = pltpu.VMEM((tm + SUBLANES, tn), F32)
    return pl.pallas_call(
        functools.partial(_up_glu_kernel, tm=tm, ts=ts, nj=nj, n_tiles=n_tiles),
        grid=(n_tiles + 2,),
        in_specs=[
            pl.BlockSpec((tm, D), lambda t: (mult(t) // nj, 0), pipeline_mode=pl.Buffered(1)),
            pl.BlockSpec((D, tn), lambda t: (0, cast(t) % nj)),
            pl.BlockSpec((D, tn), lambda t: (0, nj + cast(t) % nj)),
            pl.BlockSpec((CONV_WIDTH, tn), lambda t: (0, done(t) % nj)),
            pl.BlockSpec((CONV_WIDTH, tn), lambda t: (0, nj + done(t) % nj)),
            pl.BlockSpec((1, tn), lambda t: (0, done(t) % nj)),
            pl.BlockSpec((1, tn), lambda t: (0, nj + done(t) % nj)),
        ],
        out_specs=pl.BlockSpec((tm, tn), lambda t: (done(t) // nj, done(t) % nj)),
        out_shape=jax.ShapeDtypeStruct((S, d_ff), BF16),
        scratch_shapes=[wbuf, wbuf, wbuf, wbuf, ubuf, ubuf, ubuf, ubuf,
                        pltpu.VMEM((nj, SUBLANES, tn), F32),
                        pltpu.VMEM((nj, SUBLANES, tn), F32)],
        compiler_params=_params("arbitrary"),
        name="up_glu",
    )(h, w_up, w_up, conv_w, conv_w, conv_b, conv_b)


def _down_res_kernel(a_ref, w_ref, x_ref, o_ref, acc_ref):
    k = pl.program_id(2)

    @pl.when(k == 0)
    def _():
        acc_ref[...] = x_ref[...]

    acc_ref[...] += _dot(a_ref[...], w_ref[...])

    @pl.when(k == pl.num_programs(2) - 1)
    def _():
        o_ref[...] = acc_ref[...]


def _down_res(a, w, x, *, tm, tn, tk):
    S, K = a.shape
    N = w.shape[1]
    return pl.pallas_call(
        _down_res_kernel,
        grid=(S // tm, N // tn, K // tk),
        in_specs=[
            pl.BlockSpec((tm, tk), lambda i, j, k: (i, k)),
            pl.BlockSpec((tk, tn), lambda i, j, k: (k, j)),
            pl.BlockSpec((tm, tn), lambda i, j, k: (i, j)),
        ],
        out_specs=pl.BlockSpec((tm, tn), lambda i, j, k: (i, j)),
        out_shape=jax.ShapeDtypeStruct((S, N), F32),
        scratch_shapes=[pltpu.VMEM((tm, tn), F32)],
        compiler_params=_params("parallel", "parallel", "arbitrary"),
        name="down_res",
    )(a, w, x)


def _ple_kernel(h_ref, wg_ref, p_ref, wp_ref, x_ref, o_ref):
    h = h_ref[...]
    p = p_ref[...].astype(BF16)
    for sl in _col_blocks(o_ref.shape[1]):
        gate = jax.nn.sigmoid(_dot(h, wg_ref[:, sl]))
        o_ref[:, sl] = x_ref[:, sl] + gate * _dot(p, wp_ref[:, sl])


def _ple(h, w_gate, p, w_proj, x, *, tm, tn):
    S, D = h.shape
    P = p.shape[1]
    return pl.pallas_call(
        _ple_kernel,
        grid=(S // tm, D // tn),
        in_specs=[
            pl.BlockSpec((tm, D), lambda i, j: (i, 0)),
            pl.BlockSpec((D, tn), lambda i, j: (0, j)),
            pl.BlockSpec((tm, P), lambda i, j: (i, 0)),
            pl.BlockSpec((P, tn), lambda i, j: (0, j)),
            pl.BlockSpec((tm, tn), lambda i, j: (i, j)),
        ],
        out_specs=pl.BlockSpec((tm, tn), lambda i, j: (i, j)),
        out_shape=jax.ShapeDtypeStruct((S, D), F32),
        compiler_params=_params("parallel", "arbitrary"),
        name="ple",
    )(h, w_gate, p, w_proj, x)


def _tiles(S):
    return dict(
        norm=dict(tr=min(256, S)),
        proj=dict(tm=min(2048, S), ts=min(512, S), tn=512),
        cumsum=dict(tb=min(512, S)),
        fox=dict(tq=min(512, S), tk=min(512, S), hp=2, tb=32),
        sb=dict(tq=min(256, S), tk=min(256, S), hp=4, tb=32),
        merge=dict(tm=min(1024, S), tn=512),
        out=dict(tm=min(1024, S), tn=512),
        up=dict(tm=min(2048, S), tn=256, ts=min(512, S)),
        down=dict(tm=min(1024, S), tn=512, tk=5504),
        ple=dict(tm=min(1024, S), tn=512),
    )


def _layer(x, p, g_mix, w_in, b_f, g_q, g_k, w_bf, w_bs, w_out, g_ffn, w_up, conv_w, conv_b,
           w_down, g_ple, w_ple_gate, w_ple_proj, tiles):
    S, D = x.shape
    f0 = 3 * W_ATT
    sb0 = f0 + N_HEADS
    w_t = w_in.T
    w_f = w_t[f0:f0 + LANES].astype(BF16)
    b_f_row = jnp.pad(b_f[None, :], ((0, 0), (0, LANES - N_HEADS)))

    h, log_f = _norm(x, g_mix[None, :], w_f=w_f, b_f=b_f_row, **tiles["norm"])
    per = W_ATT // tiles["proj"]["tn"]
    proj_a = _proj("proj_fox", h, w_t, 0, ((per, "q_norm"), (per, "k_norm"), (per, "plain")),
                   (g_q[None, :], g_k[None, :]), **tiles["proj"])
    gate_tiles = (w_t.shape[0] - sb0) // tiles["proj"]["tn"] - 3 * per
    proj_b = _proj("proj_sb", h, w_t, sb0,
                   ((per, "q_scale"), (2 * per, "plain"), (gate_tiles, "sigmoid")), (),
                   **tiles["proj"])
    c = _cumsum(log_f, **tiles["cumsum"])
    tk = tiles["fox"]["tk"]
    c_rows = c[:, :N_HEADS].T.reshape(N_HEADS, S // tk, 1, tk)
    c_edges = c_rows[:, :, 0, tk - 1]
    qk_max = HEAD_DIM ** 0.5 * LOG2E * QK_SLACK * jnp.max(jnp.abs(g_q)) * jnp.max(jnp.abs(g_k))
    lim = ((UNDERFLOW + C_SLACK) * LOG2E + 2.0 * qk_max).reshape(1)
    y_a, w_bf16, w_bs16, w_out16 = _fox_attn(proj_a, c_rows, c_edges, lim, [w_bf, w_bs, w_out],
                                             **tiles["fox"])
    y_b = _sb_attn(proj_b, **tiles["sb"])
    merged, w_down16 = _merge(y_a, y_b, w_bf16, w_bs16, proj_b, [w_down], **tiles["merge"])
    x, w_gate16 = _matmul_res(merged, w_out16, x, [w_ple_gate], **tiles["out"])
    h = _norm(x, g_ffn[None, :], **tiles["norm"])
    gated = _up_glu(h, w_up, conv_w, conv_b[None, :], **tiles["up"])
    x = _down_res(gated, w_down16, x, **tiles["down"])
    h = _norm(x, g_ple[None, :], **tiles["norm"])
    x = _ple(h, w_gate16, p, w_ple_proj.astype(BF16), x, **tiles["ple"])
    return x


def _forward(x, p, g_mix, w_in, b_f, g_q_fox, g_k_fox, w_branch_fox, w_branch_sb, w_out, g_ffn,
             w_up, conv_w, conv_b, w_down, g_ple, w_ple_gate, w_ple_proj, tiles):
    B, S, D = x.shape
    outs = []
    for b in range(B):
        xb = x[b]
        for i in range(w_in.shape[0]):
            xb = _layer(xb, p[i, b], g_mix[i], w_in[i], b_f[i], g_q_fox[i], g_k_fox[i],
                        w_branch_fox[i], w_branch_sb[i], w_out[i], g_ffn[i], w_up[i], conv_w[i],
                        conv_b[i], w_down[i], g_ple[i], w_ple_gate[i], w_ple_proj[i], tiles)
        outs.append(xb)
    return jnp.stack(outs, axis=0)


def kernel(x, p, g_mix, w_in, b_f, g_q_fox, g_k_fox, w_branch_fox, w_branch_sb, w_out, g_ffn,
           w_up, conv_w, conv_b, w_down, g_ple, w_ple_gate, w_ple_proj):
    return _forward(x, p, g_mix, w_in, b_f, g_q_fox, g_k_fox, w_branch_fox, w_branch_sb, w_out,
                    g_ffn, w_up, conv_w, conv_b, w_down, g_ple, w_ple_gate, w_ple_proj,
                    _tiles(x.shape[1]))
```

```python
import functools

import jax
import jax.numpy as jnp
from jax import lax
from jax.experimental import pallas as pl
from jax.experimental.pallas import tpu as pltpu

F32 = jnp.float32
BF16 = jnp.bfloat16

EPS = 1e-6
HEAD_DIM = 128
N_HEADS = 16
W_ATT = N_HEADS * HEAD_DIM
LANES = 128
SUBLANES = 8
MXU_COLS = 256
CONV_WIDTH = 3
VMEM_LIMIT_BYTES = 56 * 1024 * 1024
UNDERFLOW = 105.0
LOG2E = 1.4426950408889634
C_SLACK = 2.0
QK_SLACK = 1.02

COL_QA, COL_KA, COL_VA = 0, 16, 32
COL_QB, COL_KB, COL_VB, COL_GA, COL_GB = 0, 16, 32, 48, 80


def _params(*sem):
    return pltpu.CompilerParams(dimension_semantics=sem, vmem_limit_bytes=VMEM_LIMIT_BYTES)


def _log_sigmoid(z):
    return jnp.minimum(z, 0.0) - jnp.log1p(jnp.exp(-jnp.abs(z)))


def _rms_norm_rows(x, g):
    ms = jnp.mean(x * x, axis=-1, keepdims=True)
    return x * lax.rsqrt(ms + EPS) * g


def _dot(a, b):
    return jnp.dot(a, b, preferred_element_type=F32)


def _dot_nt(a, b):
    return lax.dot_general(a, b, (((1,), (1,)), ((), ())), preferred_element_type=F32)


def _col_blocks(width):
    return [slice(s, s + MXU_COLS) for s in range(0, width, MXU_COLS)]


def _side_cast_specs(weights, n_steps, step_of):
    rows = LANES
    while sum(w.shape[0] // rows for w in weights) > n_steps:
        rows *= 2
    in_specs, out_specs, out_shapes, plan = [], [], [], []
    first = 0
    for w in weights:
        assert w.shape[0] % rows == 0, (w.shape, rows)
        nb = w.shape[0] // rows

        def index(*ids, first=first, nb=nb):
            return jnp.clip(step_of(*ids) - first, 0, nb - 1), 0

        in_specs.append(pl.BlockSpec((rows, w.shape[1]), index))
        out_specs.append(pl.BlockSpec((rows, w.shape[1]), index))
        out_shapes.append(jax.ShapeDtypeStruct(w.shape, BF16))
        plan.append((first, nb))
        first += nb
    return in_specs, out_specs, out_shapes, tuple(plan)


def _side_cast(step, plan, in_refs, out_refs):
    for (first, nb), i_ref, o_ref in zip(plan, in_refs, out_refs):
        @pl.when((step >= first) & (step < first + nb))
        def _():
            o_ref[...] = i_ref[...].astype(BF16)


def _norm_kernel(x_ref, g_ref, o_ref):
    o_ref[...] = _rms_norm_rows(x_ref[...], g_ref[...]).astype(BF16)


def _norm_gate_kernel(x_ref, g_ref, wf_ref, bf_ref, o_ref, lf_ref):
    h = _rms_norm_rows(x_ref[...], g_ref[...]).astype(BF16)
    o_ref[...] = h
    lf_ref[...] = _log_sigmoid(_dot_nt(h, wf_ref[...]) + bf_ref[...]) * LOG2E


def _norm(x, g, *, tr, w_f=None, b_f=None):
    S, D = x.shape
    row = pl.BlockSpec((tr, D), lambda i: (i, 0))
    gain = pl.BlockSpec((1, D), lambda i: (0, 0))
    if w_f is None:
        return pl.pallas_call(
            _norm_kernel, grid=(S // tr,), in_specs=[row, gain], out_specs=row,
            out_shape=jax.ShapeDtypeStruct((S, D), BF16),
            compiler_params=_params("parallel"), name="norm",
        )(x, g)
    return pl.pallas_call(
        _norm_gate_kernel, grid=(S // tr,),
        in_specs=[row, gain, pl.BlockSpec((LANES, D), lambda i: (0, 0)),
                  pl.BlockSpec((1, LANES), lambda i: (0, 0))],
        out_specs=[row, pl.BlockSpec((tr, LANES), lambda i: (i, 0))],
        out_shape=[jax.ShapeDtypeStruct((S, D), BF16), jax.ShapeDtypeStruct((S, LANES), F32)],
        compiler_params=_params("parallel"), name="norm_gate",
    )(x, g, w_f, b_f)


def _proj_step(h_ref, w32_ref, o_ref, wb_cast_ref, wb_mult_ref, epilogue, *, tm, ts):
    tn, d = w32_ref.shape
    kc = MXU_COLS
    for r0 in range(0, tm, ts):
        accs = [None] * (tn // MXU_COLS)
        for c in range(d // kc):
            ks = slice(c * kc, (c + 1) * kc)
            lhs = h_ref[r0:r0 + ts, ks]
            for s, sl in enumerate(_col_blocks(tn)):
                part = _dot(lhs, wb_mult_ref[ks, sl])
                accs[s] = part if accs[s] is None else accs[s] + part
            if r0 == 0:
                wb_cast_ref[ks, :] = w32_ref[:, ks].T.astype(BF16)
        for s, sl in enumerate(_col_blocks(tn)):
            o_ref[r0:r0 + ts, sl] = epilogue(accs[s]).astype(BF16)


def _proj_kernel(h_ref, w32_ref, *refs, tm, ts, nj, n_tiles, sections):
    *gain_refs, o_ref, wb0_ref, wb1_ref = refs
    wb_refs = (wb0_ref, wb1_ref)
    t = pl.program_id(0)
    j = jnp.clip(t - 1, 0, n_tiles - 1) % nj

    @pl.when(t == 0)
    def _():
        wb1_ref[...] = jnp.zeros_like(wb1_ref)

    def head_norm(g_ref, mult):
        def epilogue(acc):
            heads = [_rms_norm_rows(acc[:, c:c + HEAD_DIM], g_ref[...]) * mult
                     for c in range(0, acc.shape[1], HEAD_DIM)]
            return jnp.concatenate(heads, axis=1)
        return epilogue

    scale = HEAD_DIM ** -0.5 * LOG2E
    epilogues = {
        "q_norm": lambda: head_norm(gain_refs[0], scale),
        "k_norm": lambda: head_norm(gain_refs[1], 1.0),
        "q_scale": lambda: (lambda acc: acc * scale),
        "plain": lambda: (lambda acc: acc),
        "sigmoid": lambda: jax.nn.sigmoid,
    }
    for lo, hi, kind in sections:
        for parity in (0, 1):
            @pl.when((j >= lo) & (j < hi) & (t % 2 == parity))
            def _():
                _proj_step(h_ref, w32_ref, o_ref, wb_refs[parity], wb_refs[1 - parity],
                           epilogues[kind](), tm=tm, ts=ts)


def _proj(name, h, w_t, row0, sections, gains, *, tm, ts, tn):
    S, D = h.shape
    nj = sum(n for n, _ in sections)
    bounds, lo = [], 0
    for n, kind in sections:
        bounds.append((lo, lo + n, kind))
        lo += n
    n_tiles = (S // tm) * nj
    cast = lambda t: jnp.minimum(t, n_tiles - 1)
    mult = lambda t: jnp.clip(t - 1, 0, n_tiles - 1)
    return pl.pallas_call(
        functools.partial(_proj_kernel, tm=tm, ts=ts, nj=nj, n_tiles=n_tiles,
                          sections=tuple(bounds)),
        grid=(n_tiles + 1,),
        in_specs=[pl.BlockSpec((tm, D), lambda t: (mult(t) // nj, 0), pipeline_mode=pl.Buffered(1)),
                  pl.BlockSpec((pl.Element(tn), pl.Element(D)),
                               lambda t: (pl.multiple_of(row0 + (cast(t) % nj) * tn, N_HEADS), 0))]
                 + [pl.BlockSpec((1, HEAD_DIM), lambda t: (0, 0)) for _ in gains],
        out_specs=pl.BlockSpec((tm, tn), lambda t: (mult(t) // nj, mult(t) % nj)),
        out_shape=jax.ShapeDtypeStruct((S, nj * tn), BF16),
        scratch_shapes=[pltpu.VMEM((D, tn), BF16), pltpu.VMEM((D, tn), BF16)],
        compiler_params=_params("arbitrary"),
        name=name,
    )(h, w_t, *gains)


def _cumsum_kernel(x_ref, o_ref, carry_ref, *, tb):
    @pl.when(pl.program_id(0) == 0)
    def _():
        carry_ref[...] = jnp.zeros_like(carry_ref)

    x = x_ref[...]
    r = lax.broadcasted_iota(jnp.int32, (tb, tb), 0)
    c = lax.broadcasted_iota(jnp.int32, (tb, tb), 1)
    tri = (c <= r).astype(BF16)
    x1 = x.astype(BF16)
    r1 = x - x1.astype(F32)
    x2 = r1.astype(BF16)
    x3 = (r1 - x2.astype(F32)).astype(BF16)
    cs = _dot(tri, x1) + _dot(tri, x2) + _dot(tri, x3) + carry_ref[...]
    o_ref[...] = cs
    carry_ref[...] = cs[tb - 1:tb, :]


def _cumsum(x, *, tb):
    S, W = x.shape
    return pl.pallas_call(
        functools.partial(_cumsum_kernel, tb=tb),
        grid=(S // tb,),
        in_specs=[pl.BlockSpec((tb, W), lambda i: (i, 0))],
        out_specs=pl.BlockSpec((tb, W), lambda i: (i, 0)),
        out_shape=jax.ShapeDtypeStruct((S, W), F32),
        scratch_shapes=[pltpu.VMEM((1, W), F32)],
        compiler_params=_params("arbitrary"),
        name="cumsum_logf",
    )(x)


def _fox_kernel(edge_ref, lim_ref, q_ref, k_ref, v_ref, c_ref, *refs, tq, tk, hp, tb, side):
    side_in, (o_ref, *side_out), (m_ref, l_ref, acc_ref) = (
        refs[:len(side)], refs[len(side):2 * len(side) + 1], refs[2 * len(side) + 1:])
    g = pl.program_id(0)
    i = pl.program_id(1)
    _side_cast(g * pl.num_programs(1) + i, side, side_in, side_out)
    nd = tq // tk
    rep = tk // LANES
    m_ref[...] = jnp.full_like(m_ref, -1e30)
    l_ref[...] = jnp.zeros_like(l_ref)
    acc_ref[...] = jnp.zeros_like(acc_ref)
    row = i * tq + lax.broadcasted_iota(jnp.int32, (tb, tk), 0)
    col = lax.broadcasted_iota(jnp.int32, (tb, tk), 1)

    def step(kb, masked):
        start = pl.multiple_of(kb * tk, tk)
        for hh in range(hp):
            sl = slice(hh * HEAD_DIM, (hh + 1) * HEAD_DIM)
            s_all = _dot_nt(q_ref[:, sl], k_ref[pl.ds(start, tk), sl])
            c_row = c_ref[hh, kb]
            p_blocks = []
            for r in range(0, tq, tb):
                rows = slice(r, r + tb)
                s = s_all[rows, :] - c_row
                if masked:
                    s = jnp.where(start + col <= row + r, s, -jnp.inf)
                m_old = m_ref[hh, rows, :]
                m_new = jnp.maximum(m_old, jnp.max(s, axis=-1, keepdims=True))
                alpha = jnp.exp2(m_old - m_new)
                p = jnp.exp2(s - jnp.tile(m_new, (1, rep)))
                l_ref[hh, rows, :] = alpha * l_ref[hh, rows, :] + jnp.sum(p, axis=-1, keepdims=True)
                acc_ref[hh, rows, :] = alpha * acc_ref[hh, rows, :]
                m_ref[hh, rows, :] = m_new
                p_blocks.append(p.astype(BF16))
            p_all = jnp.concatenate(p_blocks, axis=0)
            acc_ref[hh] += _dot(p_all, v_ref[pl.ds(start, tk), sl])

    for d in range(nd):
        step((i + 1) * nd - 1 - d, True)

    top = i * nd - 1
    lim = lim_ref[0]

    def chunks_needed(hh):
        h = g * hp + hh
        m_low = jnp.min(m_ref[hh])

        def needed(kb):
            return (kb >= 0) & (-edge_ref[h, jnp.maximum(kb, 0)] - m_low > -lim)

        return top - lax.while_loop(needed, lambda kb: kb - 1, top)

    n = chunks_needed(0)
    for hh in range(1, hp):
        n = jnp.maximum(n, chunks_needed(hh))

    def body(t, carry):
        step(top - t, False)
        return carry

    lax.fori_loop(0, n, body, 0)
    for hh in range(hp):
        sl = slice(hh * HEAD_DIM, (hh + 1) * HEAD_DIM)
        o_ref[:, sl] = (acc_ref[hh] / l_ref[hh]).astype(BF16)


def _fox_attn(proj, c_rows, c_edges, lim, cast_weights, *, tq, tk, hp, tb):
    S = proj.shape[0]
    nkb = S // tk
    w = hp * HEAD_DIM
    ni = S // tq
    side_in, side_out, side_shapes, side = _side_cast_specs(
        cast_weights, (N_HEADS // hp) * ni, lambda g, i: g * ni + i)
    return pl.pallas_call(
        functools.partial(_fox_kernel, tq=tq, tk=tk, hp=hp, tb=tb, side=side),
        grid=(N_HEADS // hp, ni),
        in_specs=[
            pl.BlockSpec(memory_space=pltpu.SMEM),
            pl.BlockSpec(memory_space=pltpu.SMEM),
            pl.BlockSpec((tq, w), lambda g, i: (i, COL_QA // hp + g)),
            pl.BlockSpec((S, w), lambda g, i: (0, COL_KA // hp + g)),
            pl.BlockSpec((S, w), lambda g, i: (0, COL_VA // hp + g)),
            pl.BlockSpec((hp, nkb, 1, tk), lambda g, i: (g, 0, 0, 0)),
        ] + side_in,
        out_specs=[pl.BlockSpec((tq, w), lambda g, i: (i, g))] + side_out,
        out_shape=[jax.ShapeDtypeStruct((S, W_ATT), BF16)] + side_shapes,
        scratch_shapes=[
            pltpu.VMEM((hp, tq, LANES), F32),
            pltpu.VMEM((hp, tq, LANES), F32),
            pltpu.VMEM((hp, tq, HEAD_DIM), F32),
        ],
        compiler_params=_params("arbitrary", "arbitrary"),
        name="fox_attn",
    )(c_edges, lim, proj, proj, proj, c_rows, *cast_weights)


def _sb_kernel(q_ref, k_ref, v_ref, o_ref, run_ref, acc_ref, *, tq, tk, hp, tb):
    i = pl.program_id(1)
    nd = tq // tk
    rep = tk // LANES
    run_ref[...] = jnp.zeros_like(run_ref)
    acc_ref[...] = jnp.zeros_like(acc_ref)
    row = i * tq + lax.broadcasted_iota(jnp.int32, (tb, tk), 0)
    col = lax.broadcasted_iota(jnp.int32, (tb, tk), 1)
    jj = lax.broadcasted_iota(jnp.int32, (tk, tk), 0)
    ss = lax.broadcasted_iota(jnp.int32, (tk, tk), 1)
    later = (jj > ss).astype(BF16)

    def chunk(kb, masked):
        start = pl.multiple_of(kb * tk, tk)
        run_top = None
        for hh in range(hp):
            sl = slice(hh * HEAD_DIM, (hh + 1) * HEAD_DIM)
            z_all = _dot_nt(q_ref[:, sl], k_ref[pl.ds(start, tk), sl])
            log_betas, his, los, row_sums = [], [], [], []
            for r in range(0, tq, tb):
                z = z_all[r:r + tb, :]
                log_beta = jnp.minimum(z, 0.0) - jnp.log2(1.0 + jnp.exp2(-jnp.abs(z)))
                l1m = log_beta - z
                if masked:
                    l1m = jnp.where(start + col < row + r, l1m, 0.0)
                hi = l1m.astype(BF16)
                row_sums.append(jnp.sum(l1m, axis=-1, keepdims=True))
                log_betas.append(log_beta)
                his.append(hi)
                los.append((l1m - hi.astype(F32)).astype(BF16))
            tail_all = (_dot(jnp.concatenate(his, axis=0), later)
                        + _dot(jnp.concatenate(los, axis=0), later))
            a_blocks = []
            for b, r in enumerate(range(0, tq, tb)):
                rows = slice(r, r + tb)
                run = run_ref[hh, rows, :]
                tail = tail_all[rows, :]
                a = jnp.exp2(log_betas[b] + tail + jnp.tile(run, (1, rep)))
                if masked:
                    a = jnp.where(start + col < row + r, a, 0.0)
                a_blocks.append(a.astype(BF16))
                run_new = run + row_sums[b]
                run_ref[hh, rows, :] = run_new
                run_top = run_new if run_top is None else jnp.maximum(run_top, run_new)
            acc_ref[hh] += _dot(jnp.concatenate(a_blocks, axis=0), v_ref[pl.ds(start, tk), sl])
        return jnp.max(run_top)

    run_max = None
    for d in range(nd):
        run_max = chunk((i + 1) * nd - 1 - d, True)

    def cond(carry):
        kb, run_max = carry
        return (kb >= 0) & (run_max > -UNDERFLOW * LOG2E)

    def body(carry):
        kb, _ = carry
        return kb - 1, chunk(kb, False)

    lax.while_loop(cond, body, (i * nd - 1, run_max))
    for hh in range(hp):
        sl = slice(hh * HEAD_DIM, (hh + 1) * HEAD_DIM)
        o_ref[:, sl] = acc_ref[hh].astype(BF16)


def _sb_attn(proj, *, tq, tk, hp, tb):
    S = proj.shape[0]
    w = hp * HEAD_DIM
    return pl.pallas_call(
        functools.partial(_sb_kernel, tq=tq, tk=tk, hp=hp, tb=tb),
        grid=(N_HEADS // hp, S // tq),
        in_specs=[
            pl.BlockSpec((tq, w), lambda g, i: (i, COL_QB // hp + g)),
            pl.BlockSpec((S, w), lambda g, i: (0, COL_KB // hp + g)),
            pl.BlockSpec((S, w), lambda g, i: (0, COL_VB // hp + g)),
        ],
        out_specs=pl.BlockSpec((tq, w), lambda g, i: (i, g)),
        out_shape=jax.ShapeDtypeStruct((S, W_ATT), BF16),
        scratch_shapes=[
            pltpu.VMEM((hp, tq, LANES), F32),
            pltpu.VMEM((hp, tq, HEAD_DIM), F32),
        ],
        compiler_params=_params("parallel", "arbitrary"),
        name="sb_attn",
    )(proj, proj, proj)


def _merge_kernel(ya_ref, yb_ref, wa_ref, wb_ref, ga_ref, gb_ref, *refs, side):
    side_in, (o_ref, *side_out) = refs[:len(side)], refs[len(side):]
    _side_cast(pl.program_id(0) * pl.num_programs(1) + pl.program_id(1), side, side_in, side_out)
    ya = ya_ref[...]
    yb = yb_ref[...]
    for sl in _col_blocks(o_ref.shape[1]):
        a = _dot(ya, wa_ref[:, sl])
        b = _dot(yb, wb_ref[:, sl])
        o_ref[:, sl] = (ga_ref[:, sl].astype(F32) * a + gb_ref[:, sl].astype(F32) * b).astype(BF16)


def _merge(y_a, y_b, w_a, w_b, proj, cast_weights, *, tm, tn):
    S, K = y_a.shape
    N = w_a.shape[1]
    ga0 = COL_GA * LANES // tn
    gb0 = COL_GB * LANES // tn
    nj = N // tn
    side_in, side_out, side_shapes, side = _side_cast_specs(
        cast_weights, (S // tm) * nj, lambda i, j: i * nj + j)
    return pl.pallas_call(
        functools.partial(_merge_kernel, side=side),
        grid=(S // tm, nj),
        in_specs=[
            pl.BlockSpec((tm, K), lambda i, j: (i, 0)),
            pl.BlockSpec((tm, K), lambda i, j: (i, 0)),
            pl.BlockSpec((K, tn), lambda i, j: (0, j)),
            pl.BlockSpec((K, tn), lambda i, j: (0, j)),
            pl.BlockSpec((tm, tn), lambda i, j: (i, ga0 + j)),
            pl.BlockSpec((tm, tn), lambda i, j: (i, gb0 + j)),
        ] + side_in,
        out_specs=[pl.BlockSpec((tm, tn), lambda i, j: (i, j))] + side_out,
        out_shape=[jax.ShapeDtypeStruct((S, N), BF16)] + side_shapes,
        compiler_params=_params("arbitrary", "arbitrary"),
        name="merge",
    )(y_a, y_b, w_a, w_b, proj, proj, *cast_weights)


def _matmul_res_kernel(a_ref, w_ref, x_ref, *refs, side):
    side_in, (o_ref, *side_out) = refs[:len(side)], refs[len(side):]
    _side_cast(pl.program_id(0) * pl.num_programs(1) + pl.program_id(1), side, side_in, side_out)
    a = a_ref[...]
    for sl in _col_blocks(o_ref.shape[1]):
        o_ref[:, sl] = x_ref[:, sl] + _dot(a, w_ref[:, sl])


def _matmul_res(a, w, x, cast_weights, *, tm, tn):
    S, K = a.shape
    N = w.shape[1]
    nj = N // tn
    side_in, side_out, side_shapes, side = _side_cast_specs(
        cast_weights, (S // tm) * nj, lambda i, j: i * nj + j)
    return pl.pallas_call(
        functools.partial(_matmul_res_kernel, side=side),
        grid=(S // tm, nj),
        in_specs=[
            pl.BlockSpec((tm, K), lambda i, j: (i, 0)),
            pl.BlockSpec((K, tn), lambda i, j: (0, j)),
            pl.BlockSpec((tm, tn), lambda i, j: (i, j)),
        ] + side_in,
        out_specs=[pl.BlockSpec((tm, tn), lambda i, j: (i, j))] + side_out,
        out_shape=[jax.ShapeDtypeStruct((S, N), F32)] + side_shapes,
        compiler_params=_params("arbitrary", "arbitrary"),
        name="out_res",
    )(a, w, x, *cast_weights)


def _up_glu_kernel(h_ref, wg32_ref, wv32_ref, cwg_ref, cwv_ref, cbg_ref, cbv_ref, o_ref,
                   wg0_ref, wg1_ref, wv0_ref, wv1_ref, ug0_ref, ug1_ref, uv0_ref, uv1_ref,
                   carry_g_ref, carry_v_ref, *, tm, ts, nj, n_tiles):
    t = pl.program_id(0)
    tile = jnp.clip(t - 1, 0, n_tiles - 1)
    j = tile % nj
    wg_refs, wv_refs = (wg0_ref, wg1_ref), (wv0_ref, wv1_ref)
    ug_refs, uv_refs = (ug0_ref, ug1_ref), (uv0_ref, uv1_ref)
    tc = min(ts, 4 * SUBLANES)
    kc = h_ref.shape[1] * tc // ts
    halo = SUBLANES
    first = tile // nj == 0

    @pl.when(t == 0)
    def _():
        wg1_ref[...] = jnp.zeros_like(wg1_ref)
        wv1_ref[...] = jnp.zeros_like(wv1_ref)
        ug0_ref[...] = jnp.zeros_like(ug0_ref)
        uv0_ref[...] = jnp.zeros_like(uv0_ref)

    def cast_chunk(slot, ks):
        wg_refs[slot][ks, :] = wg32_ref[ks, :].astype(BF16)
        wv_refs[slot][ks, :] = wv32_ref[ks, :].astype(BF16)

    def epilogue_chunk(slot, r):
        def conv(u_ref, cw_ref, cb_ref):
            cw = cw_ref[...]
            rows = lambda back: u_ref[halo - back + r:halo - back + r + tc, :]
            return cb_ref[...] + cw[0:1, :] * rows(2) + cw[1:2, :] * rows(1) + cw[2:3, :] * rows(0)

        gate = conv(ug_refs[slot], cwg_ref, cbg_ref)
        val = conv(uv_refs[slot], cwv_ref, cbv_ref)
        o_ref[r:r + tc, :] = (gate * jax.nn.sigmoid(gate) * val).astype(BF16)

    def rows_step(p, r0):
        acc_g = acc_v = None
        for c in range(ts // tc):
            ks = slice(c * kc, (c + 1) * kc)
            lhs = h_ref[r0:r0 + ts, ks]
            dg = _dot(lhs, wg_refs[1 - p][ks, :])
            dv = _dot(lhs, wv_refs[1 - p][ks, :])
            acc_g = dg if acc_g is None else acc_g + dg
            acc_v = dv if acc_v is None else acc_v + dv
            if r0 == 0:
                cast_chunk(p, ks)
            epilogue_chunk(p, r0 + c * tc)
        for u, u_ref, carry_ref in ((acc_g, ug_refs[1 - p], carry_g_ref),
                                    (acc_v, uv_refs[1 - p], carry_v_ref)):
            if r0 == 0:
                u_ref[0:halo, :] = jnp.where(first, 0.0, carry_ref[j])
            u_ref[halo + r0:halo + r0 + ts, :] = u
            if r0 + ts == tm:
                carry_ref[j] = u[ts - halo:ts, :]

    for parity in (0, 1):
        @pl.when(t % 2 == parity)
        def _():
            for r0 in range(0, tm, ts):
                rows_step(parity, r0)


def _up_glu(h, w_up, conv_w, conv_b, *, tm, tn, ts):
    S, D = h.shape
    d_ff = w_up.shape[1] // 2
    nj = d_ff // tn
    n_tiles = (S // tm) * nj
    cast = lambda t: jnp.minimum(t, n_tiles - 1)
    mult = lambda t: jnp.clip(t - 1, 0, n_tiles - 1)
    done = lambda t: jnp.clip(t - 2, 0, n_tiles - 1)
    wbuf = pltpu.VMEM((D, tn), BF16)
    ubuf = pltpu.VMEM((tm + SUBLANES, tn), F32)
    return pl.pallas_call(
        functools.partial(_up_glu_kernel, tm=tm, ts=ts, nj=nj, n_tiles=n_tiles),
        grid=(n_tiles + 2,),
        in_specs=[
            pl.BlockSpec((tm, D), lambda t: (mult(t) // nj, 0), pipeline_mode=pl.Buffered(1)),
            pl.BlockSpec((D, tn), lambda t: (0, cast(t) % nj)),
            pl.BlockSpec((D, tn), lambda t: (0, nj + cast(t) % nj)),
            pl.BlockSpec((CONV_WIDTH, tn), lambda t: (0, done(t) % nj)),
            pl.BlockSpec((CONV_WIDTH, tn), lambda t: (0, nj + done(t) % nj)),
            pl.BlockSpec((1, tn), lambda t: (0, done(t) % nj)),
            pl.BlockSpec((1, tn), lambda t: (0, nj + done(t) % nj)),
        ],
        out_specs=pl.BlockSpec((tm, tn), lambda t: (done(t) // nj, done(t) % nj)),
        out_shape=jax.ShapeDtypeStruct((S, d_ff), BF16),
        scratch_shapes=[wbuf, wbuf, wbuf, wbuf, ubuf, ubuf, ubuf, ubuf,
                        pltpu.VMEM((nj, SUBLANES, tn), F32),
                        pltpu.VMEM((nj, SUBLANES, tn), F32)],
        compiler_params=_params("arbitrary"),
        name="up_glu",
    )(h, w_up, w_up, conv_w, conv_w, conv_b, conv_b)


def _down_res_kernel(a_ref, w_ref, x_ref, o_ref, acc_ref):
    k = pl.program_id(2)

    @pl.when(k == 0)
    def _():
        acc_ref[...] = x_ref[...]

    acc_ref[...] += _dot(a_ref[...], w_ref[...])

    @pl.when(k == pl.num_programs(2) - 1)
    def _():
        o_ref[...] = acc_ref[...]


def _down_res(a, w, x, *, tm, tn, tk):
    S, K = a.shape
    N = w.shape[1]
    return pl.pallas_call(
        _down_res_kernel,
        grid=(S // tm, N // tn, K // tk),
        in_specs=[
            pl.BlockSpec((tm, tk), lambda i, j, k: (i, k)),
            pl.BlockSpec((tk, tn), lambda i, j, k: (k, j)),
            pl.BlockSpec((tm, tn), lambda i, j, k: (i, j)),
        ],
        out_specs=pl.BlockSpec((tm, tn), lambda i, j, k: (i, j)),
        out_shape=jax.ShapeDtypeStruct((S, N), F32),
        scratch_shapes=[pltpu.VMEM((tm, tn), F32)],
        compiler_params=_params("parallel", "parallel", "arbitrary"),
        name="down_res",
    )(a, w, x)


def _ple_kernel(h_ref, wg_ref, p_ref, wp_ref, x_ref, o_ref):
    h = h_ref[...]
    p = p_ref[...].astype(BF16)
    for sl in _col_blocks(o_ref.shape[1]):
        gate = jax.nn.sigmoid(_dot(h, wg_ref[:, sl]))
        o_ref[:, sl] = x_ref[:, sl] + gate * _dot(p, wp_ref[:, sl])


def _ple(h, w_gate, p, w_proj, x, *, tm, tn):
    S, D = h.shape
    P = p.shape[1]
    return pl.pallas_call(
        _ple_kernel,
        grid=(S // tm, D // tn),
        in_specs=[
            pl.BlockSpec((tm, D), lambda i, j: (i, 0)),
            pl.BlockSpec((D, tn), lambda i, j: (0, j)),
            pl.BlockSpec((tm, P), lambda i, j: (i, 0)),
            pl.BlockSpec((P, tn), lambda i, j: (0, j)),
            pl.BlockSpec((tm, tn), lambda i, j: (i, j)),
        ],
        out_specs=pl.BlockSpec((tm, tn), lambda i, j: (i, j)),
        out_shape=jax.ShapeDtypeStruct((S, D), F32),
        compiler_params=_params("parallel", "arbitrary"),
        name="ple",
    )(h, w_gate, p, w_proj, x)


def _tiles(S):
    return dict(
        norm=dict(tr=min(256, S)),
        proj=dict(tm=min(2048, S), ts=min(512, S), tn=512),
        cumsum=dict(tb=min(512, S)),
        fox=dict(tq=min(512, S), tk=min(512, S), hp=2, tb=32),
        sb=dict(tq=min(256, S), tk=min(256, S), hp=4, tb=32),
        merge=dict(tm=min(1024, S), tn=512),
        out=dict(tm=min(1024, S), tn=512),
        up=dict(tm=min(2048, S), tn=256, ts=min(512, S)),
        down=dict(tm=min(1024, S), tn=512, tk=5504),
        ple=dict(tm=min(1024, S), tn=512),
    )


def _layer(x, p, g_mix, w_in, b_f, g_q, g_k, w_bf, w_bs, w_out, g_ffn, w_up, conv_w, conv_b,
           w_down, g_ple, w_ple_gate, w_ple_proj, tiles):
    S, D = x.shape
    f0 = 3 * W_ATT
    sb0 = f0 + N_HEADS
    w_t = w_in.T
    w_f = w_t[f0:f0 + LANES].astype(BF16)
    b_f_row = jnp.pad(b_f[None, :], ((0, 0), (0, LANES - N_HEADS)))

    h, log_f = _norm(x, g_mix[None, :], w_f=w_f, b_f=b_f_row, **tiles["norm"])
    per = W_ATT // tiles["proj"]["tn"]
    proj_a = _proj("proj_fox", h, w_t, 0, ((per, "q_norm"), (per, "k_norm"), (per, "plain")),
                   (g_q[None, :], g_k[None, :]), **tiles["proj"])
    gate_tiles = (w_t.shape[0] - sb0) // tiles["proj"]["tn"] - 3 * per
    proj_b = _proj("proj_sb", h, w_t, sb0,
                   ((per, "q_scale"), (2 * per, "plain"), (gate_tiles, "sigmoid")), (),
                   **tiles["proj"])
    c = _cumsum(log_f, **tiles["cumsum"])
    tk = tiles["fox"]["tk"]
    c_rows = c[:, :N_HEADS].T.reshape(N_HEADS, S // tk, 1, tk)
    c_edges = c_rows[:, :, 0, tk - 1]
    qk_max = HEAD_DIM ** 0.5 * LOG2E * QK_SLACK * jnp.max(jnp.abs(g_q)) * jnp.max(jnp.abs(g_k))
    lim = ((UNDERFLOW + C_SLACK) * LOG2E + qk_max).reshape(1)
    y_a, w_bf16, w_bs16, w_out16 = _fox_attn(proj_a, c_rows, c_edges, lim, [w_bf, w_bs, w_out],
                                             **tiles["fox"])
    y_b = _sb_attn(proj_b, **tiles["sb"])
    merged, w_down16 = _merge(y_a, y_b, w_bf16, w_bs16, proj_b, [w_down], **tiles["merge"])
    x, w_gate16 = _matmul_res(merged, w_out16, x, [w_ple_gate], **tiles["out"])
    h = _norm(x, g_ffn[None, :], **tiles["norm"])
    gated = _up_glu(h, w_up, conv_w, conv_b[None, :], **tiles["up"])
    x = _down_res(gated, w_down16, x, **tiles["down"])
    h = _norm(x, g_ple[None, :], **tiles["norm"])
    x = _ple(h, w_gate16, p, w_ple_proj.astype(BF16), x, **tiles["ple"])
    return x


def _forward(x, p, g_mix, w_in, b_f, g_q_fox, g_k_fox, w_branch_fox, w_branch_sb, w_out, g_ffn,
             w_up, conv_w, conv_b, w_down, g_ple, w_ple_gate, w_ple_proj, tiles):
    B, S, D = x.shape
    outs = []
    for b in range(B):
        xb = x[b]
        for i in range(w_in.shape[0]):
            xb = _layer(xb, p[i, b], g_mix[i], w_in[i], b_f[i], g_q_fox[i], g_k_fox[i],
                        w_branch_fox[i], w_branch_sb[i], w_out[i], g_ffn[i], w_up[i], conv_w[i],
                        conv_b[i], w_down[i], g_ple[i], w_ple_gate[i], w_ple_proj[i], tiles)
        outs.append(xb)
    return jnp.stack(outs, axis=0)


def kernel(x, p, g_mix, w_in, b_f, g_q_fox, g_k_fox, w_branch_fox, w_branch_sb, w_out, g_ffn,
           w_up, conv_w, conv_b, w_down, g_ple, w_ple_gate, w_ple_proj):
    return _forward(x, p, g_mix, w_in, b_f, g_q_fox, g_k_fox, w_branch_fox, w_branch_sb, w_out,
                    g_ffn, w_up, conv_w, conv_b, w_down, g_ple, w_ple_gate, w_ple_proj,
                    _tiles(x.shape[1]))
```

```python
import functools

import jax
import jax.numpy as jnp
from jax import lax
from jax.experimental import pallas as pl
from jax.experimental.pallas import tpu as pltpu

F32 = jnp.float32
BF16 = jnp.bfloat16

EPS = 1e-6
HEAD_DIM = 128
N_HEADS = 16
W_ATT = N_HEADS * HEAD_DIM
LANES = 128
SUBLANES = 8
MXU_COLS = 256
CONV_WIDTH = 3
VMEM_LIMIT_BYTES = 56 * 1024 * 1024
UNDERFLOW = 105.0
LOG2E = 1.4426950408889634
C_SLACK = 2.0
QK_SLACK = 1.02

COL_QA, COL_KA, COL_VA = 0, 16, 32
COL_QB, COL_KB, COL_VB, COL_GA, COL_GB = 0, 16, 32, 48, 80


def _params(*sem):
    return pltpu.CompilerParams(dimension_semantics=sem, vmem_limit_bytes=VMEM_LIMIT_BYTES)


def _log_sigmoid(z):
    return jnp.minimum(z, 0.0) - jnp.log1p(jnp.exp(-jnp.abs(z)))


def _rms_norm_rows(x, g):
    ms = jnp.mean(x * x, axis=-1, keepdims=True)
    return x * lax.rsqrt(ms + EPS) * g


def _dot(a, b):
    return jnp.dot(a, b, preferred_element_type=F32)


def _dot_nt(a, b):
    return lax.dot_general(a, b, (((1,), (1,)), ((), ())), preferred_element_type=F32)


def _col_blocks(width):
    return [slice(s, s + MXU_COLS) for s in range(0, width, MXU_COLS)]


def _side_cast_specs(weights, n_steps, step_of):
    rows = LANES
    while sum(w.shape[0] // rows for w in weights) > n_steps:
        rows *= 2
    in_specs, out_specs, out_shapes, plan = [], [], [], []
    first = 0
    for w in weights:
        assert w.shape[0] % rows == 0, (w.shape, rows)
        nb = w.shape[0] // rows

        def index(*ids, first=first, nb=nb):
            return jnp.clip(step_of(*ids) - first, 0, nb - 1), 0

        in_specs.append(pl.BlockSpec((rows, w.shape[1]), index))
        out_specs.append(pl.BlockSpec((rows, w.shape[1]), index))
        out_shapes.append(jax.ShapeDtypeStruct(w.shape, BF16))
        plan.append((first, nb))
        first += nb
    return in_specs, out_specs, out_shapes, tuple(plan)


def _side_cast(step, plan, in_refs, out_refs):
    for (first, nb), i_ref, o_ref in zip(plan, in_refs, out_refs):
        @pl.when((step >= first) & (step < first + nb))
        def _():
            o_ref[...] = i_ref[...].astype(BF16)


def _norm_kernel(x_ref, g_ref, o_ref):
    o_ref[...] = _rms_norm_rows(x_ref[...], g_ref[...]).astype(BF16)


def _norm_gate_kernel(x_ref, g_ref, wf_ref, bf_ref, o_ref, lf_ref):
    h = _rms_norm_rows(x_ref[...], g_ref[...]).astype(BF16)
    o_ref[...] = h
    lf_ref[...] = _log_sigmoid(_dot_nt(h, wf_ref[...]) + bf_ref[...]) * LOG2E


def _norm(x, g, *, tr, w_f=None, b_f=None):
    S, D = x.shape
    row = pl.BlockSpec((tr, D), lambda i: (i, 0))
    gain = pl.BlockSpec((1, D), lambda i: (0, 0))
    if w_f is None:
        return pl.pallas_call(
            _norm_kernel, grid=(S // tr,), in_specs=[row, gain], out_specs=row,
            out_shape=jax.ShapeDtypeStruct((S, D), BF16),
            compiler_params=_params("parallel"), name="norm",
        )(x, g)
    return pl.pallas_call(
        _norm_gate_kernel, grid=(S // tr,),
        in_specs=[row, gain, pl.BlockSpec((LANES, D), lambda i: (0, 0)),
                  pl.BlockSpec((1, LANES), lambda i: (0, 0))],
        out_specs=[row, pl.BlockSpec((tr, LANES), lambda i: (i, 0))],
        out_shape=[jax.ShapeDtypeStruct((S, D), BF16), jax.ShapeDtypeStruct((S, LANES), F32)],
        compiler_params=_params("parallel"), name="norm_gate",
    )(x, g, w_f, b_f)


def _proj_step(h_ref, w32_ref, o_ref, wb_cast_ref, wb_mult_ref, epilogue, *, tm, ts):
    tn, d = w32_ref.shape
    kc = MXU_COLS
    for r0 in range(0, tm, ts):
        accs = [None] * (tn // MXU_COLS)
        for c in range(d // kc):
            ks = slice(c * kc, (c + 1) * kc)
            lhs = h_ref[r0:r0 + ts, ks]
            for s, sl in enumerate(_col_blocks(tn)):
                part = _dot(lhs, wb_mult_ref[ks, sl])
                accs[s] = part if accs[s] is None else accs[s] + part
            if r0 == 0:
                wb_cast_ref[ks, :] = w32_ref[:, ks].T.astype(BF16)
        for s, sl in enumerate(_col_blocks(tn)):
            o_ref[r0:r0 + ts, sl] = epilogue(accs[s]).astype(BF16)


def _proj_kernel(h_ref, w32_ref, *refs, tm, ts, nj, n_tiles, sections):
    *gain_refs, o_ref, wb0_ref, wb1_ref = refs
    wb_refs = (wb0_ref, wb1_ref)
    t = pl.program_id(0)
    j = jnp.clip(t - 1, 0, n_tiles - 1) % nj

    @pl.when(t == 0)
    def _():
        wb1_ref[...] = jnp.zeros_like(wb1_ref)

    def head_norm(g_ref, mult):
        def epilogue(acc):
            heads = [_rms_norm_rows(acc[:, c:c + HEAD_DIM], g_ref[...]) * mult
                     for c in range(0, acc.shape[1], HEAD_DIM)]
            return jnp.concatenate(heads, axis=1)
        return epilogue

    scale = HEAD_DIM ** -0.5 * LOG2E
    epilogues = {
        "q_norm": lambda: head_norm(gain_refs[0], scale),
        "k_norm": lambda: head_norm(gain_refs[1], 1.0),
        "q_scale": lambda: (lambda acc: acc * scale),
        "plain": lambda: (lambda acc: acc),
        "sigmoid": lambda: jax.nn.sigmoid,
    }
    for lo, hi, kind in sections:
        for parity in (0, 1):
            @pl.when((j >= lo) & (j < hi) & (t % 2 == parity))
            def _():
                _proj_step(h_ref, w32_ref, o_ref, wb_refs[parity], wb_refs[1 - parity],
                           epilogues[kind](), tm=tm, ts=ts)


def _proj(name, h, w_t, row0, sections, gains, *, tm, ts, tn):
    S, D = h.shape
    nj = sum(n for n, _ in sections)
    bounds, lo = [], 0
    for n, kind in sections:
        bounds.append((lo, lo + n, kind))
        lo += n
    n_tiles = (S // tm) * nj
    cast = lambda t: jnp.minimum(t, n_tiles - 1)
    mult = lambda t: jnp.clip(t - 1, 0, n_tiles - 1)
    return pl.pallas_call(
        functools.partial(_proj_kernel, tm=tm, ts=ts, nj=nj, n_tiles=n_tiles,
                          sections=tuple(bounds)),
        grid=(n_tiles + 1,),
        in_specs=[pl.BlockSpec((tm, D), lambda t: (mult(t) // nj, 0), pipeline_mode=pl.Buffered(1)),
                  pl.BlockSpec((pl.Element(tn), pl.Element(D)),
                               lambda t: (pl.multiple_of(row0 + (cast(t) % nj) * tn, N_HEADS), 0))]
                 + [pl.BlockSpec((1, HEAD_DIM), lambda t: (0, 0)) for _ in gains],
        out_specs=pl.BlockSpec((tm, tn), lambda t: (mult(t) // nj, mult(t) % nj)),
        out_shape=jax.ShapeDtypeStruct((S, nj * tn), BF16),
        scratch_shapes=[pltpu.VMEM((D, tn), BF16), pltpu.VMEM((D, tn), BF16)],
        compiler_params=_params("arbitrary"),
        name=name,
    )(h, w_t, *gains)


def _cumsum_kernel(x_ref, o_ref, carry_ref, *, tb):
    @pl.when(pl.program_id(0) == 0)
    def _():
        carry_ref[...] = jnp.zeros_like(carry_ref)

    x = x_ref[...]
    r = lax.broadcasted_iota(jnp.int32, (tb, tb), 0)
    c = lax.broadcasted_iota(jnp.int32, (tb, tb), 1)
    tri = (c <= r).astype(BF16)
    x1 = x.astype(BF16)
    r1 = x - x1.astype(F32)
    x2 = r1.astype(BF16)
    x3 = (r1 - x2.astype(F32)).astype(BF16)
    cs = _dot(tri, x1) + _dot(tri, x2) + _dot(tri, x3) + carry_ref[...]
    o_ref[...] = cs
    carry_ref[...] = cs[tb - 1:tb, :]


def _cumsum(x, *, tb):
    S, W = x.shape
    return pl.pallas_call(
        functools.partial(_cumsum_kernel, tb=tb),
        grid=(S // tb,),
        in_specs=[pl.BlockSpec((tb, W), lambda i: (i, 0))],
        out_specs=pl.BlockSpec((tb, W), lambda i: (i, 0)),
        out_shape=jax.ShapeDtypeStruct((S, W), F32),
        scratch_shapes=[pltpu.VMEM((1, W), F32)],
        compiler_params=_params("arbitrary"),
        name="cumsum_logf",
    )(x)


def _fox_kernel(edge_ref, lim_ref, q_ref, k_ref, v_ref, c_ref, *refs, tq, tk, hp, tb, side):
    side_in, (o_ref, *side_out), (m_ref, l_ref, acc_ref) = (
        refs[:len(side)], refs[len(side):2 * len(side) + 1], refs[2 * len(side) + 1:])
    g = pl.program_id(0)
    i = pl.program_id(1)
    _side_cast(g * pl.num_programs(1) + i, side, side_in, side_out)
    nd = tq // tk
    rep = tk // LANES
    m_ref[...] = jnp.full_like(m_ref, -1e30)
    l_ref[...] = jnp.zeros_like(l_ref)
    acc_ref[...] = jnp.zeros_like(acc_ref)
    row = i * tq + lax.broadcasted_iota(jnp.int32, (tb, tk), 0)
    col = lax.broadcasted_iota(jnp.int32, (tb, tk), 1)

    def step(kb, masked):
        start = pl.multiple_of(kb * tk, tk)
        for hh in range(hp):
            sl = slice(hh * HEAD_DIM, (hh + 1) * HEAD_DIM)
            s_all = _dot_nt(q_ref[:, sl], k_ref[pl.ds(start, tk), sl])
            c_row = c_ref[hh, kb]
            p_blocks = []
            for r in range(0, tq, tb):
                rows = slice(r, r + tb)
                s = s_all[rows, :] - c_row
                if masked:
                    s = jnp.where(start + col <= row + r, s, -jnp.inf)
                m_old = m_ref[hh, rows, :]
                m_new = jnp.maximum(m_old, jnp.max(s, axis=-1, keepdims=True))
                alpha = jnp.exp2(m_old - m_new)
                p = jnp.exp2(s - jnp.tile(m_new, (1, rep)))
                l_ref[hh, rows, :] = alpha * l_ref[hh, rows, :] + jnp.sum(p, axis=-1, keepdims=True)
                acc_ref[hh, rows, :] = alpha * acc_ref[hh, rows, :]
                m_ref[hh, rows, :] = m_new
                p_blocks.append(p.astype(BF16))
            p_all = jnp.concatenate(p_blocks, axis=0)
            acc_ref[hh] += _dot(p_all, v_ref[pl.ds(start, tk), sl])

    for d in range(nd):
        step((i + 1) * nd - 1 - d, True)

    top = i * nd - 1
    lim = lim_ref[0]

    def chunks_needed(hh):
        h = g * hp + hh
        m_low = jnp.min(m_ref[hh])

        def needed(kb):
            return (kb >= 0) & (-edge_ref[h, jnp.maximum(kb, 0)] - m_low > -lim)

        return top - lax.while_loop(needed, lambda kb: kb - 1, top)

    n = chunks_needed(0)
    for hh in range(1, hp):
        n = jnp.maximum(n, chunks_needed(hh))

    def body(t, carry):
        step(top - t, False)
        return carry

    lax.fori_loop(0, n, body, 0)
    for hh in range(hp):
        sl = slice(hh * HEAD_DIM, (hh + 1) * HEAD_DIM)
        o_ref[:, sl] = (acc_ref[hh] / l_ref[hh]).astype(BF16)


def _fox_attn(proj, c_rows, c_edges, lim, cast_weights, *, tq, tk, hp, tb):
    S = proj.shape[0]
    nkb = S // tk
    w = hp * HEAD_DIM
    ni = S // tq
    side_in, side_out, side_shapes, side = _side_cast_specs(
        cast_weights, (N_HEADS // hp) * ni, lambda g, i: g * ni + i)
    return pl.pallas_call(
        functools.partial(_fox_kernel, tq=tq, tk=tk, hp=hp, tb=tb, side=side),
        grid=(N_HEADS // hp, ni),
        in_specs=[
            pl.BlockSpec(memory_space=pltpu.SMEM),
            pl.BlockSpec(memory_space=pltpu.SMEM),
            pl.BlockSpec((tq, w), lambda g, i: (i, COL_QA // hp + g)),
            pl.BlockSpec((S, w), lambda g, i: (0, COL_KA // hp + g)),
            pl.BlockSpec((S, w), lambda g, i: (0, COL_VA // hp + g)),
            pl.BlockSpec((hp, nkb, 1, tk), lambda g, i: (g, 0, 0, 0)),
        ] + side_in,
        out_specs=[pl.BlockSpec((tq, w), lambda g, i: (i, g))] + side_out,
        out_shape=[jax.ShapeDtypeStruct((S, W_ATT), BF16)] + side_shapes,
        scratch_shapes=[
            pltpu.VMEM((hp, tq, LANES), F32),
            pltpu.VMEM((hp, tq, LANES), F32),
            pltpu.VMEM((hp, tq, HEAD_DIM), F32),
        ],
        compiler_params=_params("arbitrary", "arbitrary"),
        name="fox_attn",
    )(c_edges, lim, proj, proj, proj, c_rows, *cast_weights)


def _sb_kernel(q_ref, k_ref, v_ref, o_ref, run_ref, acc_ref, *, tq, tk, hp, tb):
    i = pl.program_id(1)
    nd = tq // tk
    rep = tk // LANES
    run_ref[...] = jnp.zeros_like(run_ref)
    acc_ref[...] = jnp.zeros_like(acc_ref)
    row = i * tq + lax.broadcasted_iota(jnp.int32, (tb, tk), 0)
    col = lax.broadcasted_iota(jnp.int32, (tb, tk), 1)
    jj = lax.broadcasted_iota(jnp.int32, (tk, tk), 0)
    ss = lax.broadcasted_iota(jnp.int32, (tk, tk), 1)
    later = (jj > ss).astype(BF16)

    def chunk(kb, masked):
        start = pl.multiple_of(kb * tk, tk)
        run_top = None
        for hh in range(hp):
            sl = slice(hh * HEAD_DIM, (hh + 1) * HEAD_DIM)
            z_all = _dot_nt(q_ref[:, sl], k_ref[pl.ds(start, tk), sl])
            log_betas, his, los, row_sums = [], [], [], []
            for r in range(0, tq, tb):
                z = z_all[r:r + tb, :]
                log_beta = jnp.minimum(z, 0.0) - jnp.log2(1.0 + jnp.exp2(-jnp.abs(z)))
                l1m = log_beta - z
                if masked:
                    l1m = jnp.where(start + col < row + r, l1m, 0.0)
                hi = l1m.astype(BF16)
                row_sums.append(jnp.sum(l1m, axis=-1, keepdims=True))
                log_betas.append(log_beta)
                his.append(hi)
                los.append((l1m - hi.astype(F32)).astype(BF16))
            tail_all = (_dot(jnp.concatenate(his, axis=0), later)
                        + _dot(jnp.concatenate(los, axis=0), later))
            a_blocks = []
            for b, r in enumerate(range(0, tq, tb)):
                rows = slice(r, r + tb)
                run = run_ref[hh, rows, :]
                tail = tail_all[rows, :]
                a = jnp.exp2(log_betas[b] + tail + jnp.tile(run, (1, rep)))
                if masked:
                    a = jnp.where(start + col < row + r, a, 0.0)
                a_blocks.append(a.astype(BF16))
                run_new = run + row_sums[b]
                run_ref[hh, rows, :] = run_new
                run_top = run_new if run_top is None else jnp.maximum(run_top, run_new)
            acc_ref[hh] += _dot(jnp.concatenate(a_blocks, axis=0), v_ref[pl.ds(start, tk), sl])
        return jnp.max(run_top)

    run_max = None
    for d in range(nd):
        run_max = chunk((i + 1) * nd - 1 - d, True)

    def cond(carry):
        kb, run_max = carry
        return (kb >= 0) & (run_max > -UNDERFLOW * LOG2E)

    def body(carry):
        kb, _ = carry
        return kb - 1, chunk(kb, False)

    lax.while_loop(cond, body, (i * nd - 1, run_max))
    for hh in range(hp):
        sl = slice(hh * HEAD_DIM, (hh + 1) * HEAD_DIM)
        o_ref[:, sl] = acc_ref[hh].astype(BF16)


def _sb_attn(proj, *, tq, tk, hp, tb):
    S = proj.shape[0]
    w = hp * HEAD_DIM
    return pl.pallas_call(
        functools.partial(_sb_kernel, tq=tq, tk=tk, hp=hp, tb=tb),
        grid=(N_HEADS // hp, S // tq),
        in_specs=[
            pl.BlockSpec((tq, w), lambda g, i: (i, COL_QB // hp + g)),
            pl.BlockSpec((S, w), lambda g, i: (0, COL_KB // hp + g)),
            pl.BlockSpec((S, w), lambda g, i: (0, COL_VB // hp + g)),
        ],
        out_specs=pl.BlockSpec((tq, w), lambda g, i: (i, g)),
        out_shape=jax.ShapeDtypeStruct((S, W_ATT), BF16),
        scratch_shapes=[
            pltpu.VMEM((hp, tq, LANES), F32),
            pltpu.VMEM((hp, tq, HEAD_DIM), F32),
        ],
        compiler_params=_params("parallel", "arbitrary"),
        name="sb_attn",
    )(proj, proj, proj)


def _merge_kernel(ya_ref, yb_ref, wa_ref, wb_ref, ga_ref, gb_ref, *refs, side):
    side_in, (o_ref, *side_out) = refs[:len(side)], refs[len(side):]
    _side_cast(pl.program_id(0) * pl.num_programs(1) + pl.program_id(1), side, side_in, side_out)
    ya = ya_ref[...]
    yb = yb_ref[...]
    for sl in _col_blocks(o_ref.shape[1]):
        a = _dot(ya, wa_ref[:, sl])
        b = _dot(yb, wb_ref[:, sl])
        o_ref[:, sl] = (ga_ref[:, sl].astype(F32) * a + gb_ref[:, sl].astype(F32) * b).astype(BF16)


def _merge(y_a, y_b, w_a, w_b, proj, cast_weights, *, tm, tn):
    S, K = y_a.shape
    N = w_a.shape[1]
    ga0 = COL_GA * LANES // tn
    gb0 = COL_GB * LANES // tn
    nj = N // tn
    side_in, side_out, side_shapes, side = _side_cast_specs(
        cast_weights, (S // tm) * nj, lambda i, j: i * nj + j)
    return pl.pallas_call(
        functools.partial(_merge_kernel, side=side),
        grid=(S // tm, nj),
        in_specs=[
            pl.BlockSpec((tm, K), lambda i, j: (i, 0)),
            pl.BlockSpec((tm, K), lambda i, j: (i, 0)),
            pl.BlockSpec((K, tn), lambda i, j: (0, j)),
            pl.BlockSpec((K, tn), lambda i, j: (0, j)),
            pl.BlockSpec((tm, tn), lambda i, j: (i, ga0 + j)),
            pl.BlockSpec((tm, tn), lambda i, j: (i, gb0 + j)),
        ] + side_in,
        out_specs=[pl.BlockSpec((tm, tn), lambda i, j: (i, j))] + side_out,
        out_shape=[jax.ShapeDtypeStruct((S, N), BF16)] + side_shapes,
        compiler_params=_params("arbitrary", "arbitrary"),
        name="merge",
    )(y_a, y_b, w_a, w_b, proj, proj, *cast_weights)


def _matmul_res_kernel(a_ref, w_ref, x_ref, *refs, side):
    side_in, (o_ref, *side_out) = refs[:len(side)], refs[len(side):]
    _side_cast(pl.program_id(0) * pl.num_programs(1) + pl.program_id(1), side, side_in, side_out)
    a = a_ref[...]
    for sl in _col_blocks(o_ref.shape[1]):
        o_ref[:, sl] = x_ref[:, sl] + _dot(a, w_ref[:, sl])


def _matmul_res(a, w, x, cast_weights, *, tm, tn):
    S, K = a.shape
    N = w.shape[1]
    nj = N // tn
    side_in, side_out, side_shapes, side = _side_cast_specs(
        cast_weights, (S // tm) * nj, lambda i, j: i * nj + j)
    return pl.pallas_call(
        functools.partial(_matmul_res_kernel, side=side),
        grid=(S // tm, nj),
        in_specs=[
            pl.BlockSpec((tm, K), lambda i, j: (i, 0)),
            pl.BlockSpec((K, tn), lambda i, j: (0, j)),
            pl.BlockSpec((tm, tn), lambda i, j: (i, j)),
        ] + side_in,
        out_specs=[pl.BlockSpec((tm, tn), lambda i, j: (i, j))] + side_out,
        out_shape=[jax.ShapeDtypeStruct((S, N), F32)] + side_shapes,
        compiler_params=_params("arbitrary", "arbitrary"),
        name="out_res",
    )(a, w, x, *cast_weights)


def _up_glu_kernel(h_ref, wg32_ref, wv32_ref, cwg_ref, cwv_ref, cbg_ref, cbv_ref, o_ref,
                   wg0_ref, wg1_ref, wv0_ref, wv1_ref, ug0_ref, ug1_ref, uv0_ref, uv1_ref,
                   carry_g_ref, carry_v_ref, *, tm, ts, nj, n_tiles):
    t = pl.program_id(0)
    tile = jnp.clip(t - 1, 0, n_tiles - 1)
    j = tile % nj
    wg_refs, wv_refs = (wg0_ref, wg1_ref), (wv0_ref, wv1_ref)
    ug_refs, uv_refs = (ug0_ref, ug1_ref), (uv0_ref, uv1_ref)
    tc = min(ts, 4 * SUBLANES)
    kc = h_ref.shape[1] * tc // ts
    halo = SUBLANES
    first = tile // nj == 0

    @pl.when(t == 0)
    def _():
        wg1_ref[...] = jnp.zeros_like(wg1_ref)
        wv1_ref[...] = jnp.zeros_like(wv1_ref)
        ug0_ref[...] = jnp.zeros_like(ug0_ref)
        uv0_ref[...] = jnp.zeros_like(uv0_ref)

    def cast_chunk(slot, ks):
        wg_refs[slot][ks, :] = wg32_ref[ks, :].astype(BF16)
        wv_refs[slot][ks, :] = wv32_ref[ks, :].astype(BF16)

    def epilogue_chunk(slot, r):
        def conv(u_ref, cw_ref, cb_ref):
            cw = cw_ref[...]
            rows = lambda back: u_ref[halo - back + r:halo - back + r + tc, :]
            return cb_ref[...] + cw[0:1, :] * rows(2) + cw[1:2, :] * rows(1) + cw[2:3, :] * rows(0)

        gate = conv(ug_refs[slot], cwg_ref, cbg_ref)
        val = conv(uv_refs[slot], cwv_ref, cbv_ref)
        o_ref[r:r + tc, :] = (gate * jax.nn.sigmoid(gate) * val).astype(BF16)

    def rows_step(p, r0):
        acc_g = acc_v = None
        for c in range(ts // tc):
            ks = slice(c * kc, (c + 1) * kc)
            lhs = h_ref[r0:r0 + ts, ks]
            dg = _dot(lhs, wg_refs[1 - p][ks, :])
            dv = _dot(lhs, wv_refs[1 - p][ks, :])
            acc_g = dg if acc_g is None else acc_g + dg
            acc_v = dv if acc_v is None else acc_v + dv
            if r0 == 0:
                cast_chunk(p, ks)
            epilogue_chunk(p, r0 + c * tc)
        for u, u_ref, carry_ref in ((acc_g, ug_refs[1 - p], carry_g_ref),
                                    (acc_v, uv_refs[1 - p], carry_v_ref)):
            if r0 == 0:
                u_ref[0:halo, :] = jnp.where(first, 0.0, carry_ref[j])
            u_ref[halo + r0:halo + r0 + ts, :] = u
            if r0 + ts == tm:
                carry_ref[j] = u[ts - halo:ts, :]

    for parity in (0, 1):
        @pl.when(t % 2 == parity)
        def _():
            for r0 in range(0, tm, ts):
                rows_step(parity, r0)


def _up_glu(h, w_up, conv_w, conv_b, *, tm, tn, ts):
    S, D = h.shape
    d_ff = w_up.shape[1] // 2
    nj = d_ff // tn
    n_tiles = (S // tm) * nj
    cast = lambda t: jnp.minimum(t, n_tiles - 1)
    mult = lambda t: jnp.clip(t - 1, 0, n_tiles - 1)
    done = lambda t: jnp.clip(t - 2, 0, n_tiles - 1)
    wbuf = pltpu.VMEM((D, tn), BF16)
    ubuf = pltpu.VMEM((tm + SUBLANES, tn), F32)
    return pl.pallas_call(
        functools.partial(_up_glu_kernel, tm=tm, ts=ts, nj=nj, n_tiles=n_tiles),
        grid=(n_tiles + 2,),
        in_specs=[
            pl.BlockSpec((tm, D), lambda t: (mult(t) // nj, 0), pipeline_mode=pl.Buffered(1)),
            pl.BlockSpec((D, tn), lambda t: (0, cast(t) % nj)),
            pl.BlockSpec((D, tn), lambda t: (0, nj + cast(t) % nj)),
            pl.BlockSpec((CONV_WIDTH, tn), lambda t: (0, done(t) % nj)),
            pl.BlockSpec((CONV_WIDTH, tn), lambda t: (0, nj + done(t) % nj)),
            pl.BlockSpec((1, tn), lambda t: (0, done(t) % nj)),
            pl.BlockSpec((1, tn), lambda t: (0, nj + done(t) % nj)),
        ],
        out_specs=pl.BlockSpec((tm, tn), lambda t: (done(t) // nj, done(t) % nj)),
        out_shape=jax.ShapeDtypeStruct((S, d_ff), BF16),
        scratch_shapes=[wbuf, wbuf, wbuf, wbuf, ubuf, ubuf, ubuf, ubuf,
                        pltpu.VMEM((nj, SUBLANES, tn), F32),
                        pltpu.VMEM((nj, SUBLANES, tn), F32)],
        compiler_params=_params("arbitrary"),
        name="up_glu",
    )(h, w_up, w_up, conv_w, conv_w, conv_b, conv_b)


def _down_res_kernel(a_ref, w_ref, x_ref, o_ref, acc_ref):
    k = pl.program_id(2)

    @pl.when(k == 0)
    def _():
        acc_ref[...] = x_ref[...]

    acc_ref[...] += _dot(a_ref[...], w_ref[...])

    @pl.when(k == pl.num_programs(2) - 1)
    def _():
        o_ref[...] = acc_ref[...]


def _down_res(a, w, x, *, tm, tn, tk):
    S, K = a.shape
    N = w.shape[1]
    return pl.pallas_call(
        _down_res_kernel,
        grid=(S // tm, N // tn, K // tk),
        in_specs=[
            pl.BlockSpec((tm, tk), lambda i, j, k: (i, k)),
            pl.BlockSpec((tk, tn), lambda i, j, k: (k, j)),
            pl.BlockSpec((tm, tn), lambda i, j, k: (i, j)),
        ],
        out_specs=pl.BlockSpec((tm, tn), lambda i, j, k: (i, j)),
        out_shape=jax.ShapeDtypeStruct((S, N), F32),
        scratch_shapes=[pltpu.VMEM((tm, tn), F32)],
        compiler_params=_params("parallel", "parallel", "arbitrary"),
        name="down_res",
    )(a, w, x)


def _ple_kernel(h_ref, wg_ref, p_ref, wp_ref, x_ref, o_ref):
    h = h_ref[...]
    p = p_ref[...].astype(BF16)
    for sl in _col_blocks(o_ref.shape[1]):
        gate = jax.nn.sigmoid(_dot(h, wg_ref[:, sl]))
        o_ref[:, sl] = x_ref[:, sl] + gate * _dot(p, wp_ref[:, sl])


def _ple(h, w_gate, p, w_proj, x, *, tm, tn):
    S, D = h.shape
    P = p.shape[1]
    return pl.pallas_call(
        _ple_kernel,
        grid=(S // tm, D // tn),
        in_specs=[
            pl.BlockSpec((tm, D), lambda i, j: (i, 0)),
            pl.BlockSpec((D, tn), lambda i, j: (0, j)),
            pl.BlockSpec((tm, P), lambda i, j: (i, 0)),
            pl.BlockSpec((P, tn), lambda i, j: (0, j)),
            pl.BlockSpec((tm, tn), lambda i, j: (i, j)),
        ],
        out_specs=pl.BlockSpec((tm, tn), lambda i, j: (i, j)),
        out_shape=jax.ShapeDtypeStruct((S, D), F32),
        compiler_params=_params("parallel", "arbitrary"),
        name="ple",
    )(h, w_gate, p, w_proj, x)


def _tiles(S):
    return dict(
        norm=dict(tr=min(512, S)),
        proj=dict(tm=min(2048, S), ts=min(512, S), tn=512),
        cumsum=dict(tb=min(512, S)),
        fox=dict(tq=min(512, S), tk=min(512, S), hp=2, tb=32),
        sb=dict(tq=min(256, S), tk=min(256, S), hp=4, tb=32),
        merge=dict(tm=min(1024, S), tn=512),
        out=dict(tm=min(1024, S), tn=512),
        up=dict(tm=min(2048, S), tn=256, ts=min(512, S)),
        down=dict(tm=min(1024, S), tn=512, tk=5504),
        ple=dict(tm=min(1024, S), tn=512),
    )


def _layer(x, p, g_mix, w_in, b_f, g_q, g_k, w_bf, w_bs, w_out, g_ffn, w_up, conv_w, conv_b,
           w_down, g_ple, w_ple_gate, w_ple_proj, tiles):
    S, D = x.shape
    f0 = 3 * W_ATT
    sb0 = f0 + N_HEADS
    w_t = w_in.T
    w_f = w_t[f0:f0 + LANES].astype(BF16)
    b_f_row = jnp.pad(b_f[None, :], ((0, 0), (0, LANES - N_HEADS)))

    h, log_f = _norm(x, g_mix[None, :], w_f=w_f, b_f=b_f_row, **tiles["norm"])
    per = W_ATT // tiles["proj"]["tn"]
    proj_a = _proj("proj_fox", h, w_t, 0, ((per, "q_norm"), (per, "k_norm"), (per, "plain")),
                   (g_q[None, :], g_k[None, :]), **tiles["proj"])
    gate_tiles = (w_t.shape[0] - sb0) // tiles["proj"]["tn"] - 3 * per
    proj_b = _proj("proj_sb", h, w_t, sb0,
                   ((per, "q_scale"), (2 * per, "plain"), (gate_tiles, "sigmoid")), (),
                   **tiles["proj"])
    c = _cumsum(log_f, **tiles["cumsum"])
    tk = tiles["fox"]["tk"]
    c_rows = c[:, :N_HEADS].T.reshape(N_HEADS, S // tk, 1, tk)
    c_edges = c_rows[:, :, 0, tk - 1]
    qk_max = HEAD_DIM ** 0.5 * LOG2E * QK_SLACK * jnp.max(jnp.abs(g_q)) * jnp.max(jnp.abs(g_k))
    lim = ((UNDERFLOW + C_SLACK) * LOG2E + qk_max).reshape(1)
    y_a, w_bf16, w_bs16, w_out16 = _fox_attn(proj_a, c_rows, c_edges, lim, [w_bf, w_bs, w_out],
                                             **tiles["fox"])
    y_b = _sb_attn(proj_b, **tiles["sb"])
    merged, w_down16 = _merge(y_a, y_b, w_bf16, w_bs16, proj_b, [w_down], **tiles["merge"])
    x, w_gate16 = _matmul_res(merged, w_out16, x, [w_ple_gate], **tiles["out"])
    h = _norm(x, g_ffn[None, :], **tiles["norm"])
    gated = _up_glu(h, w_up, conv_w, conv_b[None, :], **tiles["up"])
    x = _down_res(gated, w_down16, x, **tiles["down"])
    h = _norm(x, g_ple[None, :], **tiles["norm"])
    x = _ple(h, w_gate16, p, w_ple_proj.astype(BF16), x, **tiles["ple"])
    return x


def _forward(x, p, g_mix, w_in, b_f, g_q_fox, g_k_fox, w_branch_fox, w_branch_sb, w_out, g_ffn,
             w_up, conv_w, conv_b, w_down, g_ple, w_ple_gate, w_ple_proj, tiles):
    B, S, D = x.shape
    outs = []
    for b in range(B):
        xb = x[b]
        for i in range(w_in.shape[0]):
            xb = _layer(xb, p[i, b], g_mix[i], w_in[i], b_f[i], g_q_fox[i], g_k_fox[i],
                        w_branch_fox[i], w_branch_sb[i], w_out[i], g_ffn[i], w_up[i], conv_w[i],
                        conv_b[i], w_down[i], g_ple[i], w_ple_gate[i], w_ple_proj[i], tiles)
        outs.append(xb)
    return jnp.stack(outs, axis=0)


def kernel(x, p, g_mix, w_in, b_f, g_q_fox, g_k_fox, w_branch_fox, w_branch_sb, w_out, g_ffn,
           w_up, conv_w, conv_b, w_down, g_ple, w_ple_gate, w_ple_proj):
    return _forward(x, p, g_mix, w_in, b_f, g_q_fox, g_k_fox, w_branch_fox, w_branch_sb, w_out,
                    g_ffn, w_up, conv_w, conv_b, w_down, g_ple, w_ple_gate, w_ple_proj,
                    _tiles(x.shape[1]))
```

```python
import functools

import jax
import jax.numpy as jnp
from jax import lax
from jax.experimental import pallas as pl
from jax.experimental.pallas import tpu as pltpu

F32 = jnp.float32
BF16 = jnp.bfloat16

EPS = 1e-6
HEAD_DIM = 128
N_HEADS = 16
W_ATT = N_HEADS * HEAD_DIM
LANES = 128
SUBLANES = 8
MXU_COLS = 256
CONV_WIDTH = 3
VMEM_LIMIT_BYTES = 56 * 1024 * 1024
UNDERFLOW = 105.0
LOG2E = 1.4426950408889634
C_SLACK = 2.0
QK_SLACK = 1.02

COL_QA, COL_KA, COL_VA = 0, 16, 32
COL_QB, COL_KB, COL_VB, COL_GA, COL_GB = 0, 16, 32, 48, 80


def _params(*sem):
    return pltpu.CompilerParams(dimension_semantics=sem, vmem_limit_bytes=VMEM_LIMIT_BYTES)


def _log_sigmoid(z):
    return jnp.minimum(z, 0.0) - jnp.log1p(jnp.exp(-jnp.abs(z)))


def _rms_norm_rows(x, g):
    ms = jnp.mean(x * x, axis=-1, keepdims=True)
    return x * lax.rsqrt(ms + EPS) * g


def _dot(a, b):
    return jnp.dot(a, b, preferred_element_type=F32)


def _dot_nt(a, b):
    return lax.dot_general(a, b, (((1,), (1,)), ((), ())), preferred_element_type=F32)


def _col_blocks(width):
    return [slice(s, s + MXU_COLS) for s in range(0, width, MXU_COLS)]


def _side_cast_specs(weights, n_steps, step_of):
    rows = LANES
    while sum(w.shape[0] // rows for w in weights) > n_steps:
        rows *= 2
    in_specs, out_specs, out_shapes, plan = [], [], [], []
    first = 0
    for w in weights:
        assert w.shape[0] % rows == 0, (w.shape, rows)
        nb = w.shape[0] // rows

        def index(*ids, first=first, nb=nb):
            return jnp.clip(step_of(*ids) - first, 0, nb - 1), 0

        in_specs.append(pl.BlockSpec((rows, w.shape[1]), index))
        out_specs.append(pl.BlockSpec((rows, w.shape[1]), index))
        out_shapes.append(jax.ShapeDtypeStruct(w.shape, BF16))
        plan.append((first, nb))
        first += nb
    return in_specs, out_specs, out_shapes, tuple(plan)


def _side_cast(step, plan, in_refs, out_refs):
    for (first, nb), i_ref, o_ref in zip(plan, in_refs, out_refs):
        @pl.when((step >= first) & (step < first + nb))
        def _():
            o_ref[...] = i_ref[...].astype(BF16)


def _norm_kernel(x_ref, g_ref, o_ref):
    o_ref[...] = _rms_norm_rows(x_ref[...], g_ref[...]).astype(BF16)


def _norm_gate_kernel(x_ref, g_ref, wf_ref, bf_ref, o_ref, lf_ref):
    h = _rms_norm_rows(x_ref[...], g_ref[...]).astype(BF16)
    o_ref[...] = h
    lf_ref[...] = _log_sigmoid(_dot_nt(h, wf_ref[...]) + bf_ref[...]) * LOG2E


def _norm(x, g, *, tr, w_f=None, b_f=None):
    S, D = x.shape
    row = pl.BlockSpec((tr, D), lambda i: (i, 0))
    gain = pl.BlockSpec((1, D), lambda i: (0, 0))
    if w_f is None:
        return pl.pallas_call(
            _norm_kernel, grid=(S // tr,), in_specs=[row, gain], out_specs=row,
            out_shape=jax.ShapeDtypeStruct((S, D), BF16),
            compiler_params=_params("parallel"), name="norm",
        )(x, g)
    return pl.pallas_call(
        _norm_gate_kernel, grid=(S // tr,),
        in_specs=[row, gain, pl.BlockSpec((LANES, D), lambda i: (0, 0)),
                  pl.BlockSpec((1, LANES), lambda i: (0, 0))],
        out_specs=[row, pl.BlockSpec((tr, LANES), lambda i: (i, 0))],
        out_shape=[jax.ShapeDtypeStruct((S, D), BF16), jax.ShapeDtypeStruct((S, LANES), F32)],
        compiler_params=_params("parallel"), name="norm_gate",
    )(x, g, w_f, b_f)


def _proj_step(h_ref, w32_ref, o_ref, wb_cast_ref, wb_mult_ref, epilogue, *, tm, ts):
    tn, d = w32_ref.shape
    kc = MXU_COLS
    for r0 in range(0, tm, ts):
        accs = [None] * (tn // MXU_COLS)
        for c in range(d // kc):
            ks = slice(c * kc, (c + 1) * kc)
            lhs = h_ref[r0:r0 + ts, ks]
            for s, sl in enumerate(_col_blocks(tn)):
                part = _dot(lhs, wb_mult_ref[ks, sl])
                accs[s] = part if accs[s] is None else accs[s] + part
            if r0 == 0:
                wb_cast_ref[ks, :] = w32_ref[:, ks].T.astype(BF16)
        for s, sl in enumerate(_col_blocks(tn)):
            o_ref[r0:r0 + ts, sl] = epilogue(accs[s]).astype(BF16)


def _proj_kernel(h_ref, w32_ref, *refs, tm, ts, nj, n_tiles, sections):
    *gain_refs, o_ref, wb0_ref, wb1_ref = refs
    wb_refs = (wb0_ref, wb1_ref)
    t = pl.program_id(0)
    j = jnp.clip(t - 1, 0, n_tiles - 1) % nj

    @pl.when(t == 0)
    def _():
        wb1_ref[...] = jnp.zeros_like(wb1_ref)

    def head_norm(g_ref, mult):
        def epilogue(acc):
            heads = [_rms_norm_rows(acc[:, c:c + HEAD_DIM], g_ref[...]) * mult
                     for c in range(0, acc.shape[1], HEAD_DIM)]
            return jnp.concatenate(heads, axis=1)
        return epilogue

    scale = HEAD_DIM ** -0.5 * LOG2E
    epilogues = {
        "q_norm": lambda: head_norm(gain_refs[0], scale),
        "k_norm": lambda: head_norm(gain_refs[1], 1.0),
        "q_scale": lambda: (lambda acc: acc * scale),
        "plain": lambda: (lambda acc: acc),
        "sigmoid": lambda: jax.nn.sigmoid,
    }
    for lo, hi, kind in sections:
        for parity in (0, 1):
            @pl.when((j >= lo) & (j < hi) & (t % 2 == parity))
            def _():
                _proj_step(h_ref, w32_ref, o_ref, wb_refs[parity], wb_refs[1 - parity],
                           epilogues[kind](), tm=tm, ts=ts)


def _proj(name, h, w_t, row0, sections, gains, *, tm, ts, tn):
    S, D = h.shape
    nj = sum(n for n, _ in sections)
    bounds, lo = [], 0
    for n, kind in sections:
        bounds.append((lo, lo + n, kind))
        lo += n
    n_tiles = (S // tm) * nj
    cast = lambda t: jnp.minimum(t, n_tiles - 1)
    mult = lambda t: jnp.clip(t - 1, 0, n_tiles - 1)
    return pl.pallas_call(
        functools.partial(_proj_kernel, tm=tm, ts=ts, nj=nj, n_tiles=n_tiles,
                          sections=tuple(bounds)),
        grid=(n_tiles + 1,),
        in_specs=[pl.BlockSpec((tm, D), lambda t: (mult(t) // nj, 0), pipeline_mode=pl.Buffered(1)),
                  pl.BlockSpec((pl.Element(tn), pl.Element(D)),
                               lambda t: (pl.multiple_of(row0 + (cast(t) % nj) * tn, N_HEADS), 0))]
                 + [pl.BlockSpec((1, HEAD_DIM), lambda t: (0, 0)) for _ in gains],
        out_specs=pl.BlockSpec((tm, tn), lambda t: (mult(t) // nj, mult(t) % nj)),
        out_shape=jax.ShapeDtypeStruct((S, nj * tn), BF16),
        scratch_shapes=[pltpu.VMEM((D, tn), BF16), pltpu.VMEM((D, tn), BF16)],
        compiler_params=_params("arbitrary"),
        name=name,
    )(h, w_t, *gains)


def _cumsum_kernel(x_ref, o_ref, carry_ref, *, tb):
    @pl.when(pl.program_id(0) == 0)
    def _():
        carry_ref[...] = jnp.zeros_like(carry_ref)

    x = x_ref[...]
    r = lax.broadcasted_iota(jnp.int32, (tb, tb), 0)
    c = lax.broadcasted_iota(jnp.int32, (tb, tb), 1)
    tri = (c <= r).astype(BF16)
    x1 = x.astype(BF16)
    r1 = x - x1.astype(F32)
    x2 = r1.astype(BF16)
    x3 = (r1 - x2.astype(F32)).astype(BF16)
    cs = _dot(tri, x1) + _dot(tri, x2) + _dot(tri, x3) + carry_ref[...]
    o_ref[...] = cs
    carry_ref[...] = cs[tb - 1:tb, :]


def _cumsum(x, *, tb):
    S, W = x.shape
    return pl.pallas_call(
        functools.partial(_cumsum_kernel, tb=tb),
        grid=(S // tb,),
        in_specs=[pl.BlockSpec((tb, W), lambda i: (i, 0))],
        out_specs=pl.BlockSpec((tb, W), lambda i: (i, 0)),
        out_shape=jax.ShapeDtypeStruct((S, W), F32),
        scratch_shapes=[pltpu.VMEM((1, W), F32)],
        compiler_params=_params("arbitrary"),
        name="cumsum_logf",
    )(x)


def _fox_kernel(edge_ref, lim_ref, q_ref, k_ref, v_ref, c_ref, *refs, tq, tk, hp, tb, side):
    side_in, (o_ref, *side_out), (m_ref, l_ref, acc_ref) = (
        refs[:len(side)], refs[len(side):2 * len(side) + 1], refs[2 * len(side) + 1:])
    g = pl.program_id(0)
    i = pl.program_id(1)
    _side_cast(g * pl.num_programs(1) + i, side, side_in, side_out)
    nd = tq // tk
    rep = tk // LANES
    m_ref[...] = jnp.full_like(m_ref, -1e30)
    l_ref[...] = jnp.zeros_like(l_ref)
    acc_ref[...] = jnp.zeros_like(acc_ref)
    row = i * tq + lax.broadcasted_iota(jnp.int32, (tb, tk), 0)
    col = lax.broadcasted_iota(jnp.int32, (tb, tk), 1)

    def step(kb, masked):
        start = pl.multiple_of(kb * tk, tk)
        for hh in range(hp):
            sl = slice(hh * HEAD_DIM, (hh + 1) * HEAD_DIM)
            s_all = _dot_nt(q_ref[:, sl], k_ref[pl.ds(start, tk), sl])
            c_row = c_ref[hh, kb]
            p_blocks = []
            for r in range(0, tq, tb):
                rows = slice(r, r + tb)
                s = s_all[rows, :] - c_row
                if masked:
                    s = jnp.where(start + col <= row + r, s, -jnp.inf)
                m_old = m_ref[hh, rows, :]
                m_new = jnp.maximum(m_old, jnp.max(s, axis=-1, keepdims=True))
                alpha = jnp.exp2(m_old - m_new)
                p = jnp.exp2(s - jnp.tile(m_new, (1, rep)))
                l_ref[hh, rows, :] = alpha * l_ref[hh, rows, :] + jnp.sum(p, axis=-1, keepdims=True)
                acc_ref[hh, rows, :] = alpha * acc_ref[hh, rows, :]
                m_ref[hh, rows, :] = m_new
                p_blocks.append(p.astype(BF16))
            p_all = jnp.concatenate(p_blocks, axis=0)
            acc_ref[hh] += _dot(p_all, v_ref[pl.ds(start, tk), sl])

    for d in range(nd):
        step((i + 1) * nd - 1 - d, True)

    top = i * nd - 1
    lim = lim_ref[0]

    def chunks_needed(hh):
        h = g * hp + hh
        m_low = jnp.min(m_ref[hh])

        def needed(kb):
            return (kb >= 0) & (-edge_ref[h, jnp.maximum(kb, 0)] - m_low > -lim)

        return top - lax.while_loop(needed, lambda kb: kb - 1, top)

    n = chunks_needed(0)
    for hh in range(1, hp):
        n = jnp.maximum(n, chunks_needed(hh))

    def body(t, carry):
        step(top - t, False)
        return carry

    lax.fori_loop(0, n, body, 0)
    for hh in range(hp):
        sl = slice(hh * HEAD_DIM, (hh + 1) * HEAD_DIM)
        o_ref[:, sl] = (acc_ref[hh] / l_ref[hh]).astype(BF16)


def _fox_attn(proj, c_rows, c_edges, lim, cast_weights, *, tq, tk, hp, tb):
    S = proj.shape[0]
    nkb = S // tk
    w = hp * HEAD_DIM
    ni = S // tq
    side_in, side_out, side_shapes, side = _side_cast_specs(
        cast_weights, (N_HEADS // hp) * ni, lambda g, i: g * ni + i)
    return pl.pallas_call(
        functools.partial(_fox_kernel, tq=tq, tk=tk, hp=hp, tb=tb, side=side),
        grid=(N_HEADS // hp, ni),
        in_specs=[
            pl.BlockSpec(memory_space=pltpu.SMEM),
            pl.BlockSpec(memory_space=pltpu.SMEM),
            pl.BlockSpec((tq, w), lambda g, i: (i, COL_QA // hp + g)),
            pl.BlockSpec((S, w), lambda g, i: (0, COL_KA // hp + g)),
            pl.BlockSpec((S, w), lambda g, i: (0, COL_VA // hp + g)),
            pl.BlockSpec((hp, nkb, 1, tk), lambda g, i: (g, 0, 0, 0)),
        ] + side_in,
        out_specs=[pl.BlockSpec((tq, w), lambda g, i: (i, g))] + side_out,
        out_shape=[jax.ShapeDtypeStruct((S, W_ATT), BF16)] + side_shapes,
        scratch_shapes=[
            pltpu.VMEM((hp, tq, LANES), F32),
            pltpu.VMEM((hp, tq, LANES), F32),
            pltpu.VMEM((hp, tq, HEAD_DIM), F32),
        ],
        compiler_params=_params("arbitrary", "arbitrary"),
        name="fox_attn",
    )(c_edges, lim, proj, proj, proj, c_rows, *cast_weights)


def _sb_kernel(q_ref, k_ref, v_ref, o_ref, run_ref, acc_ref, *, tq, tk, hp, tb):
    i = pl.program_id(1)
    nd = tq // tk
    rep = tk // LANES
    run_ref[...] = jnp.zeros_like(run_ref)
    acc_ref[...] = jnp.zeros_like(acc_ref)
    row = i * tq + lax.broadcasted_iota(jnp.int32, (tb, tk), 0)
    col = lax.broadcasted_iota(jnp.int32, (tb, tk), 1)
    jj = lax.broadcasted_iota(jnp.int32, (tk, tk), 0)
    ss = lax.broadcasted_iota(jnp.int32, (tk, tk), 1)
    later = (jj > ss).astype(BF16)

    def chunk(kb, masked):
        start = pl.multiple_of(kb * tk, tk)
        run_top = None
        for hh in range(hp):
            sl = slice(hh * HEAD_DIM, (hh + 1) * HEAD_DIM)
            z_all = _dot_nt(q_ref[:, sl], k_ref[pl.ds(start, tk), sl])
            log_betas, his, row_sums = [], [], []
            for r in range(0, tq, tb):
                z = z_all[r:r + tb, :]
                log_beta = jnp.minimum(z, 0.0) - jnp.log2(1.0 + jnp.exp2(-jnp.abs(z)))
                l1m = log_beta - z
                if masked:
                    l1m = jnp.where(start + col < row + r, l1m, 0.0)
                row_sums.append(jnp.sum(l1m, axis=-1, keepdims=True))
                log_betas.append(log_beta)
                his.append(l1m.astype(BF16))
            tail_all = _dot(jnp.concatenate(his, axis=0), later)
            a_blocks = []
            for b, r in enumerate(range(0, tq, tb)):
                rows = slice(r, r + tb)
                run = run_ref[hh, rows, :]
                tail = tail_all[rows, :]
                a = jnp.exp2(log_betas[b] + tail + jnp.tile(run, (1, rep)))
                if masked:
                    a = jnp.where(start + col < row + r, a, 0.0)
                a_blocks.append(a.astype(BF16))
                run_new = run + row_sums[b]
                run_ref[hh, rows, :] = run_new
                run_top = run_new if run_top is None else jnp.maximum(run_top, run_new)
            acc_ref[hh] += _dot(jnp.concatenate(a_blocks, axis=0), v_ref[pl.ds(start, tk), sl])
        return jnp.max(run_top)

    run_max = None
    for d in range(nd):
        run_max = chunk((i + 1) * nd - 1 - d, True)

    def cond(carry):
        kb, run_max = carry
        return (kb >= 0) & (run_max > -UNDERFLOW * LOG2E)

    def body(carry):
        kb, _ = carry
        return kb - 1, chunk(kb, False)

    lax.while_loop(cond, body, (i * nd - 1, run_max))
    for hh in range(hp):
        sl = slice(hh * HEAD_DIM, (hh + 1) * HEAD_DIM)
        o_ref[:, sl] = acc_ref[hh].astype(BF16)


def _sb_attn(proj, *, tq, tk, hp, tb):
    S = proj.shape[0]
    w = hp * HEAD_DIM
    return pl.pallas_call(
        functools.partial(_sb_kernel, tq=tq, tk=tk, hp=hp, tb=tb),
        grid=(N_HEADS // hp, S // tq),
        in_specs=[
            pl.BlockSpec((tq, w), lambda g, i: (i, COL_QB // hp + g)),
            pl.BlockSpec((S, w), lambda g, i: (0, COL_KB // hp + g)),
            pl.BlockSpec((S, w), lambda g, i: (0, COL_VB // hp + g)),
        ],
        out_specs=pl.BlockSpec((tq, w), lambda g, i: (i, g)),
        out_shape=jax.ShapeDtypeStruct((S, W_ATT), BF16),
        scratch_shapes=[
            pltpu.VMEM((hp, tq, LANES), F32),
            pltpu.VMEM((hp, tq, HEAD_DIM), F32),
        ],
        compiler_params=_params("parallel", "arbitrary"),
        name="sb_attn",
    )(proj, proj, proj)


def _merge_kernel(ya_ref, yb_ref, wa_ref, wb_ref, ga_ref, gb_ref, *refs, side):
    side_in, (o_ref, *side_out) = refs[:len(side)], refs[len(side):]
    _side_cast(pl.program_id(0) * pl.num_programs(1) + pl.program_id(1), side, side_in, side_out)
    ya = ya_ref[...]
    yb = yb_ref[...]
    for sl in _col_blocks(o_ref.shape[1]):
        a = _dot(ya, wa_ref[:, sl])
        b = _dot(yb, wb_ref[:, sl])
        o_ref[:, sl] = (ga_ref[:, sl].astype(F32) * a + gb_ref[:, sl].astype(F32) * b).astype(BF16)


def _merge(y_a, y_b, w_a, w_b, proj, cast_weights, *, tm, tn):
    S, K = y_a.shape
    N = w_a.shape[1]
    ga0 = COL_GA * LANES // tn
    gb0 = COL_GB * LANES // tn
    nj = N // tn
    side_in, side_out, side_shapes, side = _side_cast_specs(
        cast_weights, (S // tm) * nj, lambda i, j: i * nj + j)
    return pl.pallas_call(
        functools.partial(_merge_kernel, side=side),
        grid=(S // tm, nj),
        in_specs=[
            pl.BlockSpec((tm, K), lambda i, j: (i, 0)),
            pl.BlockSpec((tm, K), lambda i, j: (i, 0)),
            pl.BlockSpec((K, tn), lambda i, j: (0, j)),
            pl.BlockSpec((K, tn), lambda i, j: (0, j)),
            pl.BlockSpec((tm, tn), lambda i, j: (i, ga0 + j)),
            pl.BlockSpec((tm, tn), lambda i, j: (i, gb0 + j)),
        ] + side_in,
        out_specs=[pl.BlockSpec((tm, tn), lambda i, j: (i, j))] + side_out,
        out_shape=[jax.ShapeDtypeStruct((S, N), BF16)] + side_shapes,
        compiler_params=_params("arbitrary", "arbitrary"),
        name="merge",
    )(y_a, y_b, w_a, w_b, proj, proj, *cast_weights)


def _matmul_res_kernel(a_ref, w_ref, x_ref, *refs, side):
    side_in, (o_ref, *side_out) = refs[:len(side)], refs[len(side):]
    _side_cast(pl.program_id(0) * pl.num_programs(1) + pl.program_id(1), side, side_in, side_out)
    a = a_ref[...]
    for sl in _col_blocks(o_ref.shape[1]):
        o_ref[:, sl] = x_ref[:, sl] + _dot(a, w_ref[:, sl])


def _matmul_res(a, w, x, cast_weights, *, tm, tn):
    S, K = a.shape
    N = w.shape[1]
    nj = N // tn
    side_in, side_out, side_shapes, side = _side_cast_specs(
        cast_weights, (S // tm) * nj, lambda i, j: i * nj + j)
    return pl.pallas_call(
        functools.partial(_matmul_res_kernel, side=side),
        grid=(S // tm, nj),
        in_specs=[
            pl.BlockSpec((tm, K), lambda i, j: (i, 0)),
            pl.BlockSpec((K, tn), lambda i, j: (0, j)),
            pl.BlockSpec((tm, tn), lambda i, j: (i, j)),
        ] + side_in,
        out_specs=[pl.BlockSpec((tm, tn), lambda i, j: (i, j))] + side_out,
        out_shape=[jax.ShapeDtypeStruct((S, N), F32)] + side_shapes,
        compiler_params=_params("arbitrary", "arbitrary"),
        name="out_res",
    )(a, w, x, *cast_weights)


def _up_glu_kernel(h_ref, wg32_ref, wv32_ref, cwg_ref, cwv_ref, cbg_ref, cbv_ref, o_ref,
                   wg0_ref, wg1_ref, wv0_ref, wv1_ref, ug0_ref, ug1_ref, uv0_ref, uv1_ref,
                   carry_g_ref, carry_v_ref, *, tm, ts, nj, n_tiles):
    t = pl.program_id(0)
    tile = jnp.clip(t - 1, 0, n_tiles - 1)
    j = tile % nj
    wg_refs, wv_refs = (wg0_ref, wg1_ref), (wv0_ref, wv1_ref)
    ug_refs, uv_refs = (ug0_ref, ug1_ref), (uv0_ref, uv1_ref)
    tc = min(ts, 4 * SUBLANES)
    kc = h_ref.shape[1] * tc // ts
    halo = SUBLANES
    first = tile // nj == 0

    @pl.when(t == 0)
    def _():
        wg1_ref[...] = jnp.zeros_like(wg1_ref)
        wv1_ref[...] = jnp.zeros_like(wv1_ref)
        ug0_ref[...] = jnp.zeros_like(ug0_ref)
        uv0_ref[...] = jnp.zeros_like(uv0_ref)

    def cast_chunk(slot, ks):
        wg_refs[slot][ks, :] = wg32_ref[ks, :].astype(BF16)
        wv_refs[slot][ks, :] = wv32_ref[ks, :].astype(BF16)

    def epilogue_chunk(slot, r):
        def conv(u_ref, cw_ref, cb_ref):
            cw = cw_ref[...]
            rows = lambda back: u_ref[halo - back + r:halo - back + r + tc, :]
            return cb_ref[...] + cw[0:1, :] * rows(2) + cw[1:2, :] * rows(1) + cw[2:3, :] * rows(0)

        gate = conv(ug_refs[slot], cwg_ref, cbg_ref)
        val = conv(uv_refs[slot], cwv_ref, cbv_ref)
        o_ref[r:r + tc, :] = (gate * jax.nn.sigmoid(gate) * val).astype(BF16)

    def rows_step(p, r0):
        acc_g = acc_v = None
        for c in range(ts // tc):
            ks = slice(c * kc, (c + 1) * kc)
            lhs = h_ref[r0:r0 + ts, ks]
            dg = _dot(lhs, wg_refs[1 - p][ks, :])
            dv = _dot(lhs, wv_refs[1 - p][ks, :])
            acc_g = dg if acc_g is None else acc_g + dg
            acc_v = dv if acc_v is None else acc_v + dv
            if r0 == 0:
                cast_chunk(p, ks)
            epilogue_chunk(p, r0 + c * tc)
        for u, u_ref, carry_ref in ((acc_g, ug_refs[1 - p], carry_g_ref),
                                    (acc_v, uv_refs[1 - p], carry_v_ref)):
            if r0 == 0:
                u_ref[0:halo, :] = jnp.where(first, 0.0, carry_ref[j])
            u_ref[halo + r0:halo + r0 + ts, :] = u
            if r0 + ts == tm:
                carry_ref[j] = u[ts - halo:ts, :]

    for parity in (0, 1):
        @pl.when(t % 2 == parity)
        def _():
            for r0 in range(0, tm, ts):
                rows_step(parity, r0)


def _up_glu(h, w_up, conv_w, conv_b, *, tm, tn, ts):
    S, D = h.shape
    d_ff = w_up.shape[1] // 2
    nj = d_ff // tn
    n_tiles = (S // tm) * nj
    cast = lambda t: jnp.minimum(t, n_tiles - 1)
    mult = lambda t: jnp.clip(t - 1, 0, n_tiles - 1)
    done = lambda t: jnp.clip(t - 2, 0, n_tiles - 1)
    wbuf = pltpu.VMEM((D, tn), BF16)
    ubuf = pltpu.VMEM((tm + SUBLANES, tn), F32)
    return pl.pallas_call(
        functools.partial(_up_glu_kernel, tm=tm, ts=ts, nj=nj, n_tiles=n_tiles),
        grid=(n_tiles + 2,),
        in_specs=[
            pl.BlockSpec((tm, D), lambda t: (mult(t) // nj, 0), pipeline_mode=pl.Buffered(1)),
            pl.BlockSpec((D, tn), lambda t: (0, cast(t) % nj)),
            pl.BlockSpec((D, tn), lambda t: (0, nj + cast(t) % nj)),
            pl.BlockSpec((CONV_WIDTH, tn), lambda t: (0, done(t) % nj)),
            pl.BlockSpec((CONV_WIDTH, tn), lambda t: (0, nj + done(t) % nj)),
            pl.BlockSpec((1, tn), lambda t: (0, done(t) % nj)),
            pl.BlockSpec((1, tn), lambda t: (0, nj + done(t) % nj)),
        ],
        out_specs=pl.BlockSpec((tm, tn), lambda t: (done(t) // nj, done(t) % nj)),
        out_shape=jax.ShapeDtypeStruct((S, d_ff), BF16),
        scratch_shapes=[wbuf, wbuf, wbuf, wbuf, ubuf, ubuf, ubuf, ubuf,
                        pltpu.VMEM((nj, SUBLANES, tn), F32),
                        pltpu.VMEM((nj, SUBLANES, tn), F32)],
        compiler_params=_params("arbitrary"),
        name="up_glu",
    )(h, w_up, w_up, conv_w, conv_w, conv_b, conv_b)


def _down_res_kernel(a_ref, w_ref, x_ref, o_ref, acc_ref):
    k = pl.program_id(2)

    @pl.when(k == 0)
    def _():
        acc_ref[...] = x_ref[...]

    acc_ref[...] += _dot(a_ref[...], w_ref[...])

    @pl.when(k == pl.num_programs(2) - 1)
    def _():
        o_ref[...] = acc_ref[...]


def _down_res(a, w, x, *, tm, tn, tk):
    S, K = a.shape
    N = w.shape[1]
    return pl.pallas_call(
        _down_res_kernel,
        grid=(S // tm, N // tn, K // tk),
        in_specs=[
            pl.BlockSpec((tm, tk), lambda i, j, k: (i, k)),
            pl.BlockSpec((tk, tn), lambda i, j, k: (k, j)),
            pl.BlockSpec((tm, tn), lambda i, j, k: (i, j)),
        ],
        out_specs=pl.BlockSpec((tm, tn), lambda i, j, k: (i, j)),
        out_shape=jax.ShapeDtypeStruct((S, N), F32),
        scratch_shapes=[pltpu.VMEM((tm, tn), F32)],
        compiler_params=_params("parallel", "parallel", "arbitrary"),
        name="down_res",
    )(a, w, x)


def _ple_kernel(h_ref, wg_ref, p_ref, wp_ref, x_ref, o_ref):
    h = h_ref[...]
    p = p_ref[...].astype(BF16)
    for sl in _col_blocks(o_ref.shape[1]):
        gate = jax.nn.sigmoid(_dot(h, wg_ref[:, sl]))
        o_ref[:, sl] = x_ref[:, sl] + gate * _dot(p, wp_ref[:, sl])


def _ple(h, w_gate, p, w_proj, x, *, tm, tn):
    S, D = h.shape
    P = p.shape[1]
    return pl.pallas_call(
        _ple_kernel,
        grid=(S // tm, D // tn),
        in_specs=[
            pl.BlockSpec((tm, D), lambda i, j: (i, 0)),
            pl.BlockSpec((D, tn), lambda i, j: (0, j)),
            pl.BlockSpec((tm, P), lambda i, j: (i, 0)),
            pl.BlockSpec((P, tn), lambda i, j: (0, j)),
            pl.BlockSpec((tm, tn), lambda i, j: (i, j)),
        ],
        out_specs=pl.BlockSpec((tm, tn), lambda i, j: (i, j)),
        out_shape=jax.ShapeDtypeStruct((S, D), F32),
        compiler_params=_params("parallel", "arbitrary"),
        name="ple",
    )(h, w_gate, p, w_proj, x)


def _tiles(S):
    return dict(
        norm=dict(tr=min(256, S)),
        proj=dict(tm=min(2048, S), ts=min(512, S), tn=512),
        cumsum=dict(tb=min(512, S)),
        fox=dict(tq=min(512, S), tk=min(512, S), hp=2, tb=32),
        sb=dict(tq=min(256, S), tk=min(256, S), hp=4, tb=32),
        merge=dict(tm=min(1024, S), tn=512),
        out=dict(tm=min(1024, S), tn=512),
        up=dict(tm=min(2048, S), tn=256, ts=min(512, S)),
        down=dict(tm=min(1024, S), tn=512, tk=5504),
        ple=dict(tm=min(1024, S), tn=512),
    )


def _layer(x, p, g_mix, w_in, b_f, g_q, g_k, w_bf, w_bs, w_out, g_ffn, w_up, conv_w, conv_b,
           w_down, g_ple, w_ple_gate, w_ple_proj, tiles):
    S, D = x.shape
    f0 = 3 * W_ATT
    sb0 = f0 + N_HEADS
    w_t = w_in.T
    w_f = w_t[f0:f0 + LANES].astype(BF16)
    b_f_row = jnp.pad(b_f[None, :], ((0, 0), (0, LANES - N_HEADS)))

    h, log_f = _norm(x, g_mix[None, :], w_f=w_f, b_f=b_f_row, **tiles["norm"])
    per = W_ATT // tiles["proj"]["tn"]
    proj_a = _proj("proj_fox", h, w_t, 0, ((per, "q_norm"), (per, "k_norm"), (per, "plain")),
                   (g_q[None, :], g_k[None, :]), **tiles["proj"])
    gate_tiles = (w_t.shape[0] - sb0) // tiles["proj"]["tn"] - 3 * per
    proj_b = _proj("proj_sb", h, w_t, sb0,
                   ((per, "q_scale"), (2 * per, "plain"), (gate_tiles, "sigmoid")), (),
                   **tiles["proj"])
    c = _cumsum(log_f, **tiles["cumsum"])
    tk = tiles["fox"]["tk"]
    c_rows = c[:, :N_HEADS].T.reshape(N_HEADS, S // tk, 1, tk)
    c_edges = c_rows[:, :, 0, tk - 1]
    qk_max = HEAD_DIM ** 0.5 * LOG2E * QK_SLACK * jnp.max(jnp.abs(g_q)) * jnp.max(jnp.abs(g_k))
    lim = ((UNDERFLOW + C_SLACK) * LOG2E + qk_max).reshape(1)
    y_a, w_bf16, w_bs16, w_out16 = _fox_attn(proj_a, c_rows, c_edges, lim, [w_bf, w_bs, w_out],
                                             **tiles["fox"])
    y_b = _sb_attn(proj_b, **tiles["sb"])
    merged, w_down16 = _merge(y_a, y_b, w_bf16, w_bs16, proj_b, [w_down], **tiles["merge"])
    x, w_gate16 = _matmul_res(merged, w_out16, x, [w_ple_gate], **tiles["out"])
    h = _norm(x, g_ffn[None, :], **tiles["norm"])
    gated = _up_glu(h, w_up, conv_w, conv_b[None, :], **tiles["up"])
    x = _down_res(gated, w_down16, x, **tiles["down"])
    h = _norm(x, g_ple[None, :], **tiles["norm"])
    x = _ple(h, w_gate16, p, w_ple_proj.astype(BF16), x, **tiles["ple"])
    return x


def _forward(x, p, g_mix, w_in, b_f, g_q_fox, g_k_fox, w_branch_fox, w_branch_sb, w_out, g_ffn,
             w_up, conv_w, conv_b, w_down, g_ple, w_ple_gate, w_ple_proj, tiles):
    B, S, D = x.shape
    outs = []
    for b in range(B):
        xb = x[b]
        for i in range(w_in.shape[0]):
            xb = _layer(xb, p[i, b], g_mix[i], w_in[i], b_f[i], g_q_fox[i], g_k_fox[i],
                        w_branch_fox[i], w_branch_sb[i], w_out[i], g_ffn[i], w_up[i], conv_w[i],
                        conv_b[i], w_down[i], g_ple[i], w_ple_gate[i], w_ple_proj[i], tiles)
        outs.append(xb)
    return jnp.stack(outs, axis=0)


def kernel(x, p, g_mix, w_in, b_f, g_q_fox, g_k_fox, w_branch_fox, w_branch_sb, w_out, g_ffn,
           w_up, conv_w, conv_b, w_down, g_ple, w_ple_gate, w_ple_proj):
    return _forward(x, p, g_mix, w_in, b_f, g_q_fox, g_k_fox, w_branch_fox, w_branch_sb, w_out,
                    g_ffn, w_up, conv_w, conv_b, w_down, g_ple, w_ple_gate, w_ple_proj,
                    _tiles(x.shape[1]))
```
